```python
import math, functools
import jax, jax.numpy as jnp
from jax import lax
import numpy as np

D_MODEL = 1024
BATCH = 1
SEQ = 16384
DEPTH = 2
DEC_BATCH = 32
DEC_SEQ = 4
PAST_LEN = 16384
PAGE_SIZE = 128

HD_A = 64
W_A = D_MODEL // 2
H_A = W_A // HD_A
W_B = D_MODEL // 4
W_C = D_MODEL // 4
D_MIX = W_A + W_B + W_C
PROJ_IN = 3 * W_A + 2 * W_B + 3 * W_C
SPLITS = (W_A, 2 * W_A, 3 * W_A, 3 * W_A + W_B, 3 * W_A + 2 * W_B,
          3 * W_A + 2 * W_B + W_C, 3 * W_A + 2 * W_B + 2 * W_C)
MOBA_BLOCK = 256
MOBA_TOPK = 3
Q_BLOCK = 128
CONV_B = 31
CONV_C = 3
CONV_F = 3
D_FF = 11 * D_MODEL // 4
N_MEM = 256
H_X = 4
HD_X = D_MODEL // H_X
REL_BUCKETS = 32
REL_MAX_DIST = 128
ALPHA = (2 * DEPTH) ** 0.25
BETA = (8 * DEPTH) ** -0.25
LN_EPS = 1e-5

kernel_name = 'moba_conformer_shortconv_hybrid_step'


def layer_norm(x, g, b):
    xf = x.astype(jnp.float32)
    mu = xf.mean(-1, keepdims=True)
    var = jnp.square(xf - mu).mean(-1, keepdims=True)
    return ((xf - mu) * lax.rsqrt(var + LN_EPS) * g.astype(jnp.float32) + b.astype(jnp.float32)).astype(x.dtype)


def dwconv_valid(xh, w):
    return lax.conv_general_dilated(xh, w[:, None, :].astype(xh.dtype), window_strides=(1,), padding='VALID',
                                    dimension_numbers=('NWC', 'WIO', 'NWC'), feature_group_count=xh.shape[-1])


def rel_bucket(dist):
    n = jnp.maximum(dist, 0)
    max_exact = REL_BUCKETS // 2
    nf = jnp.maximum(n, max_exact).astype(jnp.float32)
    large = max_exact + (jnp.log(nf / max_exact) / math.log(REL_MAX_DIST / max_exact)
                         * (REL_BUCKETS - max_exact)).astype(jnp.int32)
    return jnp.where(n < max_exact, n, jnp.minimum(large, REL_BUCKETS - 1))


def moba_core(q, qpos, kb, vb, kmean, rel_bias):
    tq = q.shape[0]
    n_blocks = kb.shape[0]
    n_sel = min(MOBA_TOPK, n_blocks)
    qblk = qpos // MOBA_BLOCK
    gate = jnp.einsum('thd,nhd->thn', q, kmean, preferred_element_type=jnp.float32)
    fully_past = jnp.arange(n_blocks)[None, None, :] < qblk[:, None, None]
    _, sel = lax.top_k(jnp.where(fully_past, gate, -jnp.inf), n_sel)
    sel_ok = sel < qblk[:, None, None]
    hidx = jnp.arange(H_A)[None, :, None]
    k_sel = kb[sel, :, hidx]
    v_sel = vb[sel, :, hidx]
    k_own = kb[qblk]
    v_own = vb[qblk]
    offs = jnp.arange(MOBA_BLOCK)
    bias_hb = rel_bias.astype(jnp.float32).T
    pos_sel = sel[..., None] * MOBA_BLOCK + offs
    l_sel = jnp.einsum('thd,thskd->thsk', q, k_sel, preferred_element_type=jnp.float32)
    l_sel = l_sel + bias_hb[hidx[..., None], rel_bucket(qpos[:, None, None, None] - pos_sel)]
    l_sel = jnp.where(sel_ok[..., None], l_sel, -jnp.inf)
    pos_own = qblk[:, None] * MOBA_BLOCK + offs
    l_own = jnp.einsum('thd,tkhd->thk', q, k_own, preferred_element_type=jnp.float32)
    l_own = l_own + bias_hb[:, rel_bucket(qpos[:, None] - pos_own)].transpose(1, 0, 2)
    l_own = jnp.where((pos_own <= qpos[:, None])[:, None, :], l_own, -jnp.inf)
    probs = jax.nn.softmax(jnp.concatenate([l_sel.reshape(tq, H_A, n_sel * MOBA_BLOCK), l_own], -1), axis=-1)
    probs = probs.astype(v_sel.dtype)
    p_sel = probs[..., :n_sel * MOBA_BLOCK].reshape(tq, H_A, n_sel, MOBA_BLOCK)
    p_own = probs[..., n_sel * MOBA_BLOCK:]
    return jnp.einsum('thsk,thskd->thd', p_sel, v_sel) + jnp.einsum('thk,tkhd->thd', p_own, v_own)


def to_blocks(*parts):
    b = parts[0].shape[0]
    t = sum(p.shape[1] for p in parts)
    nb = -(-t // MOBA_BLOCK)
    pad = jnp.zeros((b, nb * MOBA_BLOCK - t, H_A, HD_A), parts[0].dtype)
    return jnp.concatenate(parts + (pad,), axis=1).reshape(b, nb, MOBA_BLOCK, H_A, HD_A)


def block_means(kb):
    return jnp.mean(kb, axis=2, dtype=jnp.float32).astype(kb.dtype)


def prompt_attention(q, k, v, rel_bias):
    b, t = q.shape[:2]
    kb, vb = to_blocks(k), to_blocks(v)
    kmean = block_means(kb)
    n_qb = t // Q_BLOCK
    q_blocks = q.reshape(b, n_qb, Q_BLOCK, H_A, HD_A)
    pos = jnp.arange(t, dtype=jnp.int32).reshape(n_qb, Q_BLOCK)

    def one_seq(qs, kbs, vbs, kms):
        return lax.map(lambda a: moba_core(a[0], a[1], kbs, vbs, kms, rel_bias), (qs, pos))

    return jax.vmap(one_seq)(q_blocks, kb, vb, kmean).reshape(b, t, W_A)


def sample_attention(q, k, v, past_k, past_v, rel_bias):
    b, t = q.shape[:2]
    past_len = past_k.shape[1]
    kb, vb = to_blocks(past_k, k), to_blocks(past_v, v)
    kmean = block_means(kb)
    qpos = past_len + jnp.arange(t, dtype=jnp.int32)
    out = jax.vmap(moba_core, in_axes=(0, None, 0, 0, 0, None))(q, qpos, kb, vb, kmean, rel_bias)
    return out.reshape(b, t, W_A)


def trunk_layer(x, attn_fn, mem_k, mem_v, hist_b, hist_c, hist_f, p):
    b, t, _ = x.shape
    z = x @ p['w_in']
    q, k, v, glu_v, glu_g, gate_out, gate_in, h_c = jnp.split(z, SPLITS, axis=-1)
    k = k.reshape(b, t, H_A, HD_A)
    v = v.reshape(b, t, H_A, HD_A)
    y_a = attn_fn(q.reshape(b, t, H_A, HD_A) * HD_A ** -0.5, k, v)
    u_b = jnp.concatenate([hist_b, glu_v * jax.nn.sigmoid(glu_g)], axis=1)
    c_b = dwconv_valid(u_b, p['convb_w']) + p['convb_b']
    y_b = jax.nn.silu(layer_norm(c_b, p['lnb_g'], p['lnb_b']))
    u_c = jnp.concatenate([hist_c, gate_in * h_c], axis=1)
    y_c = gate_out * dwconv_valid(u_c, p['convc_w'])
    mix = jnp.concatenate([y_a, y_b, y_c], axis=-1) @ p['w_out']
    x = layer_norm(ALPHA * x + mix, p['ln1_g'], p['ln1_b'])
    qx = (x @ p['w_qx']).reshape(b, t, H_X, HD_X) * HD_X ** -0.5
    s = jnp.einsum('bthd,bmhd->bhtm', qx, mem_k, preferred_element_type=jnp.float32)
    pr = jax.nn.softmax(s, axis=-1).astype(x.dtype)
    ox = jnp.einsum('bhtm,bmhd->bthd', pr, mem_v).reshape(b, t, H_X * HD_X) @ p['w_ox']
    x = layer_norm(ALPHA * x + ox, p['ln2_g'], p['ln2_b'])
    u_f = jnp.concatenate([hist_f, x @ p['w_gate']], axis=1)
    g_f = dwconv_valid(u_f, p['convf_w']) + p['convf_b']
    y_f = (jax.nn.silu(g_f) * (x @ p['w_up'])) @ p['w_down']
    x = layer_norm(ALPHA * x + y_f, p['ln3_g'], p['ln3_b'])
    return x, u_b[:, -(CONV_B - 1):], u_c[:, -(CONV_C - 1):], u_f[:, -(CONV_F - 1):], k, v


def setup_inputs(seed: int = 0) -> dict:
    key = jax.random.key(seed)
    keys = jax.random.split(key, 34)

    def nrm(i, shape, scale=1.0):
        return scale * jax.random.normal(keys[i], shape, jnp.float32)

    n_pages = PAST_LEN // PAGE_SIZE
    n_used = DEC_BATCH * n_pages
    n_pool = n_used + -(-n_used // 4)
    page_table = jax.random.permutation(keys[5], n_pool)[:n_used].reshape(DEC_BATCH, n_pages).astype(jnp.int32)
    return {
        'x_prompt': nrm(0, (BATCH, SEQ, D_MODEL)),
        'x_sample': nrm(1, (DEC_BATCH, DEC_SEQ, D_MODEL)),
        'mem_prompt': nrm(2, (BATCH, N_MEM, D_MODEL)),
        'cache_k': nrm(3, (n_pool, DEPTH, PAGE_SIZE, H_A, HD_A)),
        'cache_v': nrm(4, (n_pool, DEPTH, PAGE_SIZE, H_A, HD_A)),
        'page_table': page_table,
        'cache_mem_k': nrm(6, (DEPTH, DEC_BATCH, N_MEM, H_X, HD_X)),
        'cache_mem_v': nrm(7, (DEPTH, DEC_BATCH, N_MEM, H_X, HD_X)),
        'state_conv_b': nrm(8, (DEPTH, DEC_BATCH, CONV_B - 1, W_B), 0.5),
        'state_conv_c': nrm(9, (DEPTH, DEC_BATCH, CONV_C - 1, W_C), 0.5),
        'state_conv_f': nrm(10, (DEPTH, DEC_BATCH, CONV_F - 1, D_FF)),
        'rel_bias': nrm(11, (REL_BUCKETS, H_A), 0.5),
        'w_in': nrm(12, (DEPTH, D_MODEL, PROJ_IN), D_MODEL ** -0.5),
        'w_out': nrm(13, (DEPTH, D_MIX, D_MODEL), BETA * D_MIX ** -0.5),
        'convb_w': nrm(14, (DEPTH, CONV_B, W_B), CONV_B ** -0.5),
        'convb_b': nrm(15, (DEPTH, W_B), 0.02),
        'lnb_g': 1.0 + nrm(16, (DEPTH, W_B), 0.02),
        'lnb_b': nrm(17, (DEPTH, W_B), 0.02),
        'convc_w': nrm(18, (DEPTH, CONV_C, W_C), CONV_C ** -0.5),
        'ln1_g': 1.0 + nrm(19, (DEPTH, D_MODEL), 0.02),
        'ln1_b': nrm(20, (DEPTH, D_MODEL), 0.02),
        'w_qx': nrm(21, (DEPTH, D_MODEL, H_X * HD_X), D_MODEL ** -0.5),
        'w_kx': nrm(22, (DEPTH, D_MODEL, H_X * HD_X), D_MODEL ** -0.5),
        'w_vx': nrm(23, (DEPTH, D_MODEL, H_X * HD_X), D_MODEL ** -0.5),
        'w_ox': nrm(24, (DEPTH, H_X * HD_X, D_MODEL), BETA * (H_X * HD_X) ** -0.5),
        'ln2_g': 1.0 + nrm(25, (DEPTH, D_MODEL), 0.02),
        'ln2_b': nrm(26, (DEPTH, D_MODEL), 0.02),
        'w_gate': nrm(27, (DEPTH, D_MODEL, D_FF), D_MODEL ** -0.5),
        'w_up': nrm(28, (DEPTH, D_MODEL, D_FF), D_MODEL ** -0.5),
        'convf_w': nrm(29, (DEPTH, CONV_F, D_FF), CONV_F ** -0.5),
        'convf_b': nrm(30, (DEPTH, D_FF), 0.02),
        'w_down': nrm(31, (DEPTH, D_FF, D_MODEL), BETA * D_FF ** -0.5),
        'ln3_g': 1.0 + nrm(32, (DEPTH, D_MODEL), 0.02),
        'ln3_b': nrm(33, (DEPTH, D_MODEL), 0.02),
    }


def reference(x_prompt, x_sample, mem_prompt, cache_k, cache_v, page_table, cache_mem_k, cache_mem_v,
              state_conv_b, state_conv_c, state_conv_f, rel_bias, w_in, w_out, convb_w, convb_b, lnb_g, lnb_b,
              convc_w, ln1_g, ln1_b, w_qx, w_kx, w_vx, w_ox, ln2_g, ln2_b, w_gate, w_up, convf_w, convf_b,
              w_down, ln3_g, ln3_b):
    bp, bs = x_prompt.shape[0], x_sample.shape[0]
    n_mem = mem_prompt.shape[1]
    past_len = page_table.shape[1] * PAGE_SIZE
    attn_p = functools.partial(prompt_attention, rel_bias=rel_bias)
    xp, xs = x_prompt, x_sample
    kp, vp, ks, vs, mkp, mvp, cbp, cbs, ccp, ccs, cfp, cfs = ([] for _ in range(12))
    for l in range(DEPTH):
        lw = {'w_in': w_in[l], 'w_out': w_out[l], 'convb_w': convb_w[l], 'convb_b': convb_b[l],
              'lnb_g': lnb_g[l], 'lnb_b': lnb_b[l], 'convc_w': convc_w[l], 'ln1_g': ln1_g[l], 'ln1_b': ln1_b[l],
              'w_qx': w_qx[l], 'w_ox': w_ox[l], 'ln2_g': ln2_g[l], 'ln2_b': ln2_b[l], 'w_gate': w_gate[l],
              'w_up': w_up[l], 'convf_w': convf_w[l], 'convf_b': convf_b[l], 'w_down': w_down[l],
              'ln3_g': ln3_g[l], 'ln3_b': ln3_b[l]}
        mk = (mem_prompt @ w_kx[l]).reshape(bp, n_mem, H_X, HD_X)
        mv = (mem_prompt @ w_vx[l]).reshape(bp, n_mem, H_X, HD_X)
        xp, hb, hc, hf, k_new, v_new = trunk_layer(
            xp, attn_p, mk, mv,
            jnp.zeros((bp, CONV_B - 1, W_B), xp.dtype), jnp.zeros((bp, CONV_C - 1, W_C), xp.dtype),
            jnp.zeros((bp, CONV_F - 1, D_FF), xp.dtype), lw)
        kp.append(k_new); vp.append(v_new); mkp.append(mk); mvp.append(mv)
        cbp.append(hb); ccp.append(hc); cfp.append(hf)
        past_k = cache_k[page_table, l].reshape(bs, past_len, H_A, HD_A)
        past_v = cache_v[page_table, l].reshape(bs, past_len, H_A, HD_A)
        attn_s = functools.partial(sample_attention, past_k=past_k, past_v=past_v, rel_bias=rel_bias)
        xs, hb, hc, hf, k_new, v_new = trunk_layer(
            xs, attn_s, cache_mem_k[l], cache_mem_v[l], state_conv_b[l], state_conv_c[l], state_conv_f[l], lw)
        ks.append(k_new); vs.append(v_new)
        cbs.append(hb); ccs.append(hc); cfs.append(hf)
    k_prompt = jnp.stack(kp, axis=1)
    v_prompt = jnp.stack(vp, axis=1)
    k_sample = jnp.stack(ks, axis=1)
    v_sample = jnp.stack(vs, axis=1)
    mem_k_prompt = jnp.stack(mkp, axis=0)
    mem_v_prompt = jnp.stack(mvp, axis=0)
    conv_b_prompt = jnp.stack(cbp, axis=0)
    conv_b_sample = jnp.stack(cbs, axis=0)
    conv_c_prompt = jnp.stack(ccp, axis=0)
    conv_c_sample = jnp.stack(ccs, axis=0)
    conv_f_prompt = jnp.stack(cfp, axis=0)
    conv_f_sample = jnp.stack(cfs, axis=0)
    return (xp, xs, k_prompt, v_prompt, k_sample, v_sample, mem_k_prompt, mem_v_prompt,
            conv_b_prompt, conv_b_sample, conv_c_prompt, conv_c_sample, conv_f_prompt, conv_f_sample)
```

```python
import functools
import math

import numpy as np
import jax
import jax.numpy as jnp
from jax import lax
from jax.experimental import pallas as pl
from jax.experimental.pallas import tpu as pltpu

F32 = jnp.float32
BF16 = jnp.bfloat16

D_MODEL = 1024
PAGE_SIZE = 128
HD_A = 64
W_A = 512
H_A = 8
W_B = 256
W_C = 256
PROJ_IN = 2816
MOBA_BLOCK = 256
MOBA_TOPK = 3
CONV_B = 31
CONV_C = 3
CONV_F = 3
D_FF = 2816
FF_CHUNK = 256
H_X = 4
HD_X = 256
REL_BUCKETS = 32
REL_MAX_DIST = 128
LN_EPS = 1e-5

NEG = -1e30
V7X_VMEM_LIMIT = 56 * 1024 * 1024


def _alpha(depth):
    return (2 * depth) ** 0.25


def _cparams(sem):
    return pltpu.CompilerParams(dimension_semantics=sem, vmem_limit_bytes=V7X_VMEM_LIMIT)


def _full(shape):
    return pl.BlockSpec(shape, lambda *_: (0,) * len(shape))


def _sigmoid(x):
    return 1.0 / (1.0 + jnp.exp(-x))


def _layer_norm(h, g, b):
    mu = jnp.mean(h, axis=-1, keepdims=True)
    d = h - mu
    var = jnp.mean(d * d, axis=-1, keepdims=True)
    return d * lax.rsqrt(var + LN_EPS) * g + b


def _dot(a, b):
    return jnp.dot(a, b, preferred_element_type=F32)


def _dot_nt(a, b):
    return lax.dot_general(a, b, (((1,), (1,)), ((), ())), preferred_element_type=F32)


def _rel_bucket_np(dist):
    n = np.maximum(dist, 0)
    max_exact = REL_BUCKETS // 2
    nf = np.maximum(n, max_exact).astype(np.float32)
    large = max_exact + (np.log(nf / np.float32(max_exact)) / np.float32(math.log(REL_MAX_DIST / max_exact))
                         * np.float32(REL_BUCKETS - max_exact)).astype(np.int32)
    return np.where(n < max_exact, n, np.minimum(large, REL_BUCKETS - 1)).astype(np.int32)


def _in_proj_kernel(x_ref, w_ref, *outs, prompt, tm):
    xb = x_ref[...].astype(BF16)

    def proj(lo, hi):
        return _dot(xb, w_ref[:, lo:hi])

    q = proj(0, W_A) * (HD_A ** -0.5)
    k = proj(W_A, 2 * W_A)
    v = proj(2 * W_A, 3 * W_A)
    o = 3 * W_A
    glu = proj(o, o + W_B) * _sigmoid(proj(o + W_B, o + 2 * W_B))
    o += 2 * W_B
    gate_out = proj(o, o + W_C)
    uc = proj(o + W_C, o + 2 * W_C) * proj(o + 2 * W_C, o + 3 * W_C)
    if prompt:
        qt_ref, kb_ref, vt_ref, k_ref, v_ref, glu_ref, uc_ref, go_ref, km_ref = outs
        for c in range(W_A // 128):
            qt_ref[c * 128:(c + 1) * 128, :] = q[:, c * 128:(c + 1) * 128].T.astype(BF16)
        kb_ref[...] = k.astype(BF16)
        for s in range(tm // MOBA_BLOCK):
            rows = slice(s * MOBA_BLOCK, (s + 1) * MOBA_BLOCK)
            for c in range(W_A // 128):
                vt_ref[s, c * 128:(c + 1) * 128, :] = v[rows, c * 128:(c + 1) * 128].T.astype(BF16)
            km_ref[s] = jnp.sum(k[rows, :], axis=0, keepdims=True) * (1.0 / MOBA_BLOCK)
    else:
        q_ref, k_ref, v_ref, glu_ref, uc_ref, go_ref = outs
        q_ref[...] = q
    k_ref[...] = k
    v_ref[...] = v
    glu_ref[...] = glu
    uc_ref[...] = uc
    go_ref[...] = gate_out


def _in_proj(x, w_bf, *, prompt):
    t = x.shape[0]
    tm = min(512, t)
    assert t % tm == 0
    nb = t // MOBA_BLOCK
    row = lambda n: pl.BlockSpec((tm, n), lambda i: (i, 0))
    sd = jax.ShapeDtypeStruct
    if prompt:
        assert tm % MOBA_BLOCK == 0
        spb = tm // MOBA_BLOCK
        out_shape = [sd((W_A, t), BF16), sd((t, W_A), BF16), sd((nb, W_A, MOBA_BLOCK), BF16),
                     sd((t, W_A), F32), sd((t, W_A), F32), sd((t, W_B), F32), sd((t, W_C), F32),
                     sd((t, W_C), F32), sd((nb, 1, W_A), F32)]
        out_specs = [pl.BlockSpec((W_A, tm), lambda i: (0, i)), row(W_A),
                     pl.BlockSpec((spb, W_A, MOBA_BLOCK), lambda i: (i, 0, 0)),
                     row(W_A), row(W_A), row(W_B), row(W_C), row(W_C),
                     pl.BlockSpec((spb, 1, W_A), lambda i: (i, 0, 0))]
    else:
        out_shape = [sd((t, W_A), F32), sd((t, W_A), F32), sd((t, W_A), F32), sd((t, W_B), F32),
                     sd((t, W_C), F32), sd((t, W_C), F32)]
        out_specs = [row(W_A), row(W_A), row(W_A), row(W_B), row(W_C), row(W_C)]
    return pl.pallas_call(
        functools.partial(_in_proj_kernel, prompt=prompt, tm=tm),
        grid=(t // tm,),
        in_specs=[row(D_MODEL), _full((D_MODEL, PROJ_IN))],
        out_specs=out_specs, out_shape=out_shape,
        compiler_params=_cparams(("arbitrary",)),
        name="in_proj_prompt" if prompt else "in_proj_sample",
    )(x, w_bf)


def _moba_prompt_kernel(rb_ref, bko_ref, bkp_ref, qt_ref, k_ref, vt_ref, km_ref, o_ref,
                        bown, bprev, sel_ref, m_ref, l_ref, acc_ref, *, nb):
    p = pl.program_id(0)
    i = pl.program_id(1)
    blk = MOBA_BLOCK

    @pl.when(i == 0)
    def _():
        bko = bko_ref[...]
        bkp = bkp_ref[...]
        for hh in range(2):
            h = 2 * p + hh
            far = rb_ref[REL_BUCKETS - 1, h]
            bo = jnp.full((blk, blk), NEG, F32)
            bp = jnp.zeros((blk, blk), F32)
            for b in range(REL_BUCKETS):
                val = rb_ref[b, h] - far
                bo = jnp.where(bko == b, val, bo)
                bp = jnp.where(bkp == b, val, bp)
            bown[hh] = bo
            bprev[hh] = bp

    qt = qt_ref[...]
    rowi = lax.broadcasted_iota(jnp.int32, qt.shape, 0)
    zero = jnp.zeros_like(qt)
    qts = [jnp.where(rowi < HD_A, qt, zero), jnp.where(rowi >= HD_A, qt, zero)]

    km = km_ref[...].astype(BF16)
    bidx = lax.broadcasted_iota(jnp.int32, (nb, blk), 0)
    for hh in range(2):
        g = _dot(km, qts[hh])
        g = jnp.where(bidx < i, g, -jnp.inf)
        selected = jnp.zeros((nb, blk), jnp.bool_)
        for _ in range(MOBA_TOPK):
            mx = jnp.max(g, axis=0, keepdims=True)
            first = jnp.min(jnp.where(g == mx, bidx, nb), axis=0, keepdims=True)
            hit = (bidx == first) & (mx > -jnp.inf)
            selected = selected | hit
            g = jnp.where(hit, -jnp.inf, g)
        sel_ref[hh] = jnp.where(selected, 0.0, NEG)

    def scores(j, hh):
        return _dot(k_ref[j], qts[hh])

    def pv(j, hh, pt):
        return _dot(vt_ref[j, hh * HD_A:(hh + 1) * HD_A, :], pt)

    for hh in range(2):
        s = scores(i, hh) + bown[hh]
        m = jnp.max(s, axis=0, keepdims=True)
        pt = jnp.exp(s - m)
        l_ref[hh] = jnp.sum(pt, axis=0, keepdims=True)
        m_ref[hh] = m
        acc_ref[hh] = pv(i, hh, pt.astype(BF16))

    def update(j, hh, bias):
        s = scores(j, hh) + bias
        m_old = m_ref[hh]
        m_new = jnp.maximum(m_old, jnp.max(s, axis=0, keepdims=True))
        a = jnp.exp(m_old - m_new)
        pt = jnp.exp(s - m_new)
        l_ref[hh] = a * l_ref[hh] + jnp.sum(pt, axis=0, keepdims=True)
        acc_ref[hh] = a * acc_ref[hh] + pv(j, hh, pt.astype(BF16))
        m_ref[hh] = m_new

    @pl.when(i >= 1)
    def _():
        for hh in range(2):
            update(i - 1, hh, bprev[hh] + sel_ref[hh, pl.ds(i - 1, 1), :])

    def far_block(j, carry):
        for hh in range(2):
            update(j, hh, sel_ref[hh, pl.ds(j, 1), :])
        return carry

    lax.fori_loop(0, jnp.maximum(i - 1, 0), far_block, 0)

    ot = jnp.concatenate([acc_ref[0] / l_ref[0], acc_ref[1] / l_ref[1]], axis=0)
    o_ref[...] = ot.T


def _moba_prompt(rel_bias, qt, kb, vt, kmean):
    t = qt.shape[1]
    nb = t // MOBA_BLOCK
    blk = MOBA_BLOCK
    c = np.arange(blk)[:, None]
    r = np.arange(blk)[None, :]
    bk_own = np.where(c <= r, _rel_bucket_np(r - c), -1).astype(np.int32)
    bk_prev = _rel_bucket_np(blk + r - c)
    smem = pl.BlockSpec(memory_space=pltpu.SMEM)
    return pl.pallas_call(
        functools.partial(_moba_prompt_kernel, nb=nb),
        grid=(H_A // 2, nb),
        in_specs=[smem, _full((blk, blk)), _full((blk, blk)),
                  pl.BlockSpec((128, blk), lambda p, i: (p, i)),
                  pl.BlockSpec((nb, blk, 128), lambda p, i: (0, 0, p)),
                  pl.BlockSpec((nb, 128, blk), lambda p, i: (0, p, 0)),
                  pl.BlockSpec((nb, 128), lambda p, i: (0, p))],
        out_specs=pl.BlockSpec((blk, 128), lambda p, i: (i, p)),
        out_shape=jax.ShapeDtypeStruct((t, W_A), F32),
        scratch_shapes=[pltpu.VMEM((2, blk, blk), F32), pltpu.VMEM((2, blk, blk), F32),
                        pltpu.VMEM((2, nb, blk), F32), pltpu.VMEM((2, 1, blk), F32),
                        pltpu.VMEM((2, 1, blk), F32), pltpu.VMEM((2, HD_A, blk), F32)],
        compiler_params=_cparams(("arbitrary", "arbitrary")),
        name="moba_prompt",
    )(rel_bias, jnp.asarray(bk_own), jnp.asarray(bk_prev), qt, kb.reshape(nb, blk, W_A), vt,
      kmean.reshape(nb, W_A))


def _head_rows(x, n_tok):
    lane_head = lax.broadcasted_iota(jnp.int32, (H_A, W_A), 1) // HD_A
    sub = lax.broadcasted_iota(jnp.int32, (H_A, W_A), 0)
    parts = [jnp.where(lane_head == sub, jnp.broadcast_to(x[t:t + 1, :], (H_A, W_A)), 0.0)
             for t in range(n_tok)]
    return jnp.concatenate(parts, axis=0)


def _moba_sample_k_kernel(pt_ref, q_ref, kn_ref, rbr_ref, bkl_ref, *rest, u, nchunk, ds):
    kpages = rest[:u]
    p_ref, st_ref = rest[u], rest[u + 1]
    qbd_ref, s_ref, ks_ref = rest[u + 2:]
    c = pl.program_id(1)
    nrow = ds * H_A
    bpc = u // 2
    nbp = nchunk * bpc

    @pl.when(c == 0)
    def _():
        qbd_ref[...] = _head_rows(q_ref[...], ds).astype(BF16)

    qbd = qbd_ref[...]
    for v in range(bpc):
        k0 = kpages[2 * v][...]
        k1 = kpages[2 * v + 1][...]
        ks_ref[pl.ds(c * bpc + v, 1), :] = (jnp.sum(k0, axis=0, keepdims=True)
                                           + jnp.sum(k1, axis=0, keepdims=True))
        s_ref[c, :, (2 * v) * PAGE_SIZE:(2 * v + 1) * PAGE_SIZE] = _dot_nt(qbd, k0.astype(BF16))
        s_ref[c, :, (2 * v + 1) * PAGE_SIZE:(2 * v + 2) * PAGE_SIZE] = _dot_nt(qbd, k1.astype(BF16))

    @pl.when(c == nchunk - 1)
    def _():
        blk = MOBA_BLOCK
        km = (ks_ref[...] * (1.0 / blk)).astype(BF16)
        g = _dot_nt(qbd, km)
        lane = lax.broadcasted_iota(jnp.int32, g.shape, 1)
        selected = jnp.zeros(g.shape, jnp.bool_)
        for _ in range(min(MOBA_TOPK, nbp)):
            mx = jnp.max(g, axis=1, keepdims=True)
            first = jnp.min(jnp.where(g == mx, lane, nbp), axis=1, keepdims=True)
            hit = (lane == first) & (mx > -jnp.inf)
            selected = selected | hit
            g = jnp.where(hit, -jnp.inf, g)
        maskadd = jnp.where(selected, 0.0, NEG)

        rbr = rbr_ref[...]
        far = rbr[:, REL_BUCKETS - 1:REL_BUCKETS]
        bkl = bkl_ref[...]
        bias_last = jnp.zeros((nrow, blk), F32)
        for b in range(REL_BUCKETS):
            bias_last = jnp.where(bkl == b, rbr[:, b:b + 1], bias_last)

        trow = lax.broadcasted_iota(jnp.int32, (nrow, 1), 0) // H_A
        qf = qbd.astype(F32)
        knb = kn_ref[...].astype(BF16).astype(F32)
        own = []
        for cc in range(ds):
            sc = jnp.sum(qf * knb[cc:cc + 1, :], axis=1, keepdims=True)
            bias = jnp.zeros((nrow, 1), F32)
            for d in range(ds - cc):
                bias = jnp.where(trow - cc == d, rbr[:, d:d + 1], bias)
            own.append(jnp.where(trow >= cc, sc + bias, NEG))

        def logits(n):
            cn, off = n // bpc, (n % bpc) * blk
            b = bias_last if n == nbp - 1 else far
            return s_ref[cn, :, off:off + blk] + b + maskadd[:, n:n + 1]

        mt = logits(0)
        for n in range(1, nbp):
            mt = jnp.maximum(mt, logits(n))
        m = jnp.max(mt, axis=1, keepdims=True)
        for cc in range(ds):
            m = jnp.maximum(m, own[cc])
        lt = jnp.zeros((nrow, blk), F32)
        for n in range(nbp):
            cn, off = n // bpc, (n % bpc) * blk
            pn = jnp.exp(logits(n) - m)
            lt = lt + pn
            p_ref[cn, :, off:off + blk] = pn.astype(BF16)
        l = jnp.sum(lt, axis=1, keepdims=True)
        lane128 = lax.broadcasted_iota(jnp.int32, (nrow, 128), 1)
        stats = jnp.zeros((nrow, 128), F32)
        for cc in range(ds):
            po = jnp.exp(own[cc] - m)
            l = l + po
            stats = jnp.where(lane128 == cc, po, stats)
        st_ref[...] = jnp.where(lane128 == ds, l, stats)


def _moba_sample_v_kernel(pt_ref, p_ref, st_ref, vn_ref, *rest, u, nchunk, ds):
    vpages = rest[:u]
    o_ref, acc_ref = rest[u], rest[u + 1]
    c = pl.program_id(1)

    @pl.when(c == 0)
    def _():
        acc_ref[...] = jnp.zeros_like(acc_ref)

    acc = acc_ref[...]
    for j in range(u):
        acc = acc + _dot(p_ref[:, j * PAGE_SIZE:(j + 1) * PAGE_SIZE], vpages[j][...].astype(BF16))
    acc_ref[...] = acc

    @pl.when(c == nchunk - 1)
    def _():
        a = acc_ref[...]
        st = st_ref[...]
        vnb = vn_ref[...].astype(BF16).astype(F32)
        for cc in range(ds):
            a = a + st[:, cc:cc + 1].astype(BF16).astype(F32) * vnb[cc:cc + 1, :]
        a = a / st[:, ds:ds + 1]
        lane_head = lax.broadcasted_iota(jnp.int32, (H_A, W_A), 1) // HD_A
        sub = lax.broadcasted_iota(jnp.int32, (H_A, W_A), 0)
        for t in range(ds):
            rows = a[t * H_A:(t + 1) * H_A, :]
            o_ref[t:t + 1, :] = jnp.sum(jnp.where(lane_head == sub, rows, 0.0), axis=0, keepdims=True)


def _moba_sample(layer, rel_bias, page_table, cache_k, cache_v, q, k_new, v_new):
    b, ds, _ = q.shape
    n_pages = page_table.shape[1]
    past_len = n_pages * PAGE_SIZE
    assert past_len % MOBA_BLOCK == 0 and ds <= PAGE_SIZE - 1
    nbp = past_len // MOBA_BLOCK
    u = min(16, n_pages)
    assert n_pages % u == 0 and u % 2 == 0
    nchunk = n_pages // u
    nrow = ds * H_A
    n_pool, depth = cache_k.shape[:2]
    ck = cache_k.reshape(n_pool, depth, PAGE_SIZE, W_A)
    cv = cache_v.reshape(n_pool, depth, PAGE_SIZE, W_A)
    rbr = jnp.tile(rel_bias.T, (ds, 1))
    trow = np.arange(nrow)[:, None] // H_A
    bk_last = _rel_bucket_np(MOBA_BLOCK + trow - np.arange(MOBA_BLOCK)[None, :])

    def page_spec(j):
        return pl.BlockSpec((None, None, PAGE_SIZE, W_A),
                            lambda bi, c, pt: (pt[bi, c * u + j], layer, 0, 0))

    per_seq = lambda r, n: pl.BlockSpec((None, r, n), lambda bi, c, pt: (bi, 0, 0))
    const = lambda r, n: pl.BlockSpec((r, n), lambda bi, c, pt: (0, 0))
    probs, stats = pl.pallas_call(
        functools.partial(_moba_sample_k_kernel, u=u, nchunk=nchunk, ds=ds),
        grid_spec=pltpu.PrefetchScalarGridSpec(
            num_scalar_prefetch=1, grid=(b, nchunk),
            in_specs=[per_seq(ds, W_A), per_seq(ds, W_A), const(nrow, REL_BUCKETS), const(nrow, MOBA_BLOCK)]
                     + [page_spec(j) for j in range(u)],
            out_specs=[pl.BlockSpec((None, nchunk, nrow, u * PAGE_SIZE), lambda bi, c, pt: (bi, 0, 0, 0)),
                       per_seq(nrow, 128)],
            scratch_shapes=[pltpu.VMEM((nrow, W_A), BF16), pltpu.VMEM((nchunk, nrow, u * PAGE_SIZE), F32),
                            pltpu.VMEM((nbp, W_A), F32)]),
        out_shape=[jax.ShapeDtypeStruct((b, nchunk, nrow, u * PAGE_SIZE), BF16),
                   jax.ShapeDtypeStruct((b, nrow, 128), F32)],
        compiler_params=_cparams(("arbitrary", "arbitrary")),
        name="moba_sample_keys",
    )(page_table, q, k_new, rbr, jnp.asarray(bk_last), *([ck] * u))
    return pl.pallas_call(
        functools.partial(_moba_sample_v_kernel, u=u, nchunk=nchunk, ds=ds),
        grid_spec=pltpu.PrefetchScalarGridSpec(
            num_scalar_prefetch=1, grid=(b, nchunk),
            in_specs=[pl.BlockSpec((None, None, nrow, u * PAGE_SIZE), lambda bi, c, pt: (bi, c, 0, 0)),
                      per_seq(nrow, 128), per_seq(ds, W_A)] + [page_spec(j) for j in range(u)],
            out_specs=per_seq(ds, W_A),
            scratch_shapes=[pltpu.VMEM((nrow, W_A), F32)]),
        out_shape=jax.ShapeDtypeStruct((b, ds, W_A), F32),
        compiler_params=_cparams(("arbitrary", "arbitrary")),
        name="moba_sample_values",
    )(page_table, probs, stats, v_new, *([cv] * u))


def _mix_kernel(x_ref, ya_ref, glu_ref, uc_ref, go_ref, hb_ref, hc_ref, cbw_ref, cbb_ref, lbg_ref, lbb_ref,
                ccw_ref, wo_ref, g1_ref, b1_ref, o_ref, extb, extc, *, tm, shift, hbb, hbc, alpha):
    @pl.when(pl.program_id(0) == 0)
    def _():
        extb[0:hbb, :] = hb_ref[...]
        extc[0:hbc, :] = hc_ref[...]

    extb[hbb:hbb + tm, :] = glu_ref[...]
    extc[hbc:hbc + tm, :] = uc_ref[...]
    offb = hbb - (CONV_B - 1) * shift
    cb = jnp.broadcast_to(cbb_ref[...], (tm, W_B))
    for w in range(CONV_B):
        cb = cb + extb[offb + w * shift:offb + w * shift + tm, :] * cbw_ref[w:w + 1, :]
    yb = _layer_norm(cb, lbg_ref[...], lbb_ref[...])
    yb = yb * _sigmoid(yb)
    offc = hbc - (CONV_C - 1) * shift
    cc = extc[offc:offc + tm, :] * ccw_ref[0:1, :]
    for w in range(1, CONV_C):
        cc = cc + extc[offc + w * shift:offc + w * shift + tm, :] * ccw_ref[w:w + 1, :]
    yc = go_ref[...] * cc
    mix = (_dot(ya_ref[...].astype(BF16), wo_ref[0:W_A, :])
           + _dot(yb.astype(BF16), wo_ref[W_A:W_A + W_B, :])
           + _dot(yc.astype(BF16), wo_ref[W_A + W_B:W_A + W_B + W_C, :]))
    o_ref[...] = _layer_norm(alpha * x_ref[...] + mix, g1_ref[...], b1_ref[...])
    nb_rows = extb[tm:tm + hbb, :]
    nc_rows = extc[tm:tm + hbc, :]
    extb[0:hbb, :] = nb_rows
    extc[0:hbc, :] = nc_rows


def _mix(x, ya, glu, uc, go, hist_b, hist_c, lw, *, shift, alpha):
    t = x.shape[0]
    tm = min(512, t)
    assert t % tm == 0
    hbb, hbc = hist_b.shape[0], hist_c.shape[0]
    row = lambda n: pl.BlockSpec((tm, n), lambda i: (i, 0))
    return pl.pallas_call(
        functools.partial(_mix_kernel, tm=tm, shift=shift, hbb=hbb, hbc=hbc, alpha=alpha),
        grid=(t // tm,),
        in_specs=[row(D_MODEL), row(W_A), row(W_B), row(W_C), row(W_C), _full((hbb, W_B)), _full((hbc, W_C)),
                  _full((CONV_B, W_B)), _full((1, W_B)), _full((1, W_B)), _full((1, W_B)), _full((CONV_C, W_C)),
                  _full((D_MODEL, D_MODEL)), _full((1, D_MODEL)), _full((1, D_MODEL))],
        out_specs=row(D_MODEL),
        out_shape=jax.ShapeDtypeStruct((t, D_MODEL), F32),
        scratch_shapes=[pltpu.VMEM((hbb + tm, W_B), F32), pltpu.VMEM((hbc + tm, W_C), F32)],
        compiler_params=_cparams(("arbitrary",)),
        name="mix_ln1",
    )(x, ya, glu, uc, go, hist_b, hist_c, lw['convb_w'], lw['convb_b'], lw['lnb_g'], lw['lnb_b'],
      lw['convc_w'], lw['w_out'], lw['ln1_g'], lw['ln1_b'])


def _cross_kernel(x_ref, mk_ref, mv_ref, wq_ref, wo_ref, g_ref, b_ref, o_ref, qx_ref, acc_ref,
                  *, tm, n_groups, alpha):
    g = pl.program_id(1)

    @pl.when(g == 0)
    def _():
        qx_ref[...] = (_dot(x_ref[...].astype(BF16), wq_ref[...]) * (HD_X ** -0.5)).astype(BF16)
        acc_ref[...] = jnp.zeros_like(acc_ref)

    mk = mk_ref[...].astype(BF16)
    mv = mv_ref[...].astype(BF16)
    heads = []
    for h in range(H_X):
        cols = slice(h * HD_X, (h + 1) * HD_X)
        s = _dot_nt(qx_ref[:, cols], mk[:, cols])
        s = s - jnp.max(s, axis=-1, keepdims=True)
        e = jnp.exp(s)
        pr = e / jnp.sum(e, axis=-1, keepdims=True)
        heads.append(_dot(pr.astype(BF16), mv[:, cols]))
    o = jnp.concatenate(heads, axis=-1)
    if n_groups == 1:
        acc_ref[...] = o
    else:
        rgrp = (pl.program_id(0) * tm + lax.broadcasted_iota(jnp.int32, (tm, 1), 0)) % n_groups
        acc_ref[...] = jnp.where(rgrp == g, o, acc_ref[...])

    @pl.when(g == n_groups - 1)
    def _():
        ox = _dot(acc_ref[...].astype(BF16), wo_ref[...])
        o_ref[...] = _layer_norm(alpha * x_ref[...] + ox, g_ref[...], b_ref[...])


def _cross(x, mem_k, mem_v, lw, *, alpha):
    t = x.shape[0]
    n_groups, n_mem, _ = mem_k.shape
    tm = min(512, t)
    assert t % tm == 0
    row = pl.BlockSpec((tm, D_MODEL), lambda i, g: (i, 0))
    mem = pl.BlockSpec((None, n_mem, D_MODEL), lambda i, g: (g, 0, 0))
    cst = lambda r, n: pl.BlockSpec((r, n), lambda i, g: (0, 0))
    return pl.pallas_call(
        functools.partial(_cross_kernel, tm=tm, n_groups=n_groups, alpha=alpha),
        grid=(t // tm, n_groups),
        in_specs=[row, mem, mem, cst(D_MODEL, D_MODEL), cst(D_MODEL, D_MODEL), cst(1, D_MODEL), cst(1, D_MODEL)],
        out_specs=row,
        out_shape=jax.ShapeDtypeStruct((t, D_MODEL), F32),
        scratch_shapes=[pltpu.VMEM((tm, D_MODEL), BF16), pltpu.VMEM((tm, D_MODEL), F32)],
        compiler_params=_cparams(("arbitrary", "arbitrary")),
        name="cross_ln2",
    )(x, mem_k, mem_v, lw['w_qx'], lw['w_ox'], lw['ln2_g'], lw['ln2_b'])


def _ffn_kernel(x_ref, hf_ref, wg_ref, wu_ref, cfw_ref, cfb_ref, wd_ref, g_ref, b_ref, o_ref, st_ref,
                carry, ext, acc_ref, *, tm, shift, hbf, alpha):
    @pl.when(pl.program_id(0) == 0)
    def _():
        carry[...] = hf_ref[...]

    xb = x_ref[...].astype(BF16)
    off = hbf - (CONV_F - 1) * shift
    for c in range(D_FF // FF_CHUNK):
        cols = slice(c * FF_CHUNK, (c + 1) * FF_CHUNK)
        ext[0:hbf, :] = carry[:, cols]
        ext[hbf:hbf + tm, :] = _dot(xb, wg_ref[:, cols])
        gf = jnp.broadcast_to(cfb_ref[:, cols], (tm, FF_CHUNK))
        for w in range(CONV_F):
            gf = gf + ext[off + w * shift:off + w * shift + tm, :] * cfw_ref[w:w + 1, cols]
        carry[:, cols] = ext[tm:tm + hbf, :]
        hid = gf * _sigmoid(gf) * _dot(xb, wu_ref[:, cols])
        part = _dot(hid.astype(BF16), wd_ref[cols, :])
        if c == 0:
            acc_ref[...] = part
        else:
            acc_ref[...] += part
    o_ref[...] = _layer_norm(alpha * x_ref[...] + acc_ref[...], g_ref[...], b_ref[...])
    st_ref[...] = carry[...]


def _ffn(x, hist_f, lw, *, shift, alpha):
    t = x.shape[0]
    tm = min(512, t)
    assert t % tm == 0
    hbf = hist_f.shape[0]
    row = pl.BlockSpec((tm, D_MODEL), lambda i: (i, 0))
    return pl.pallas_call(
        functools.partial(_ffn_kernel, tm=tm, shift=shift, hbf=hbf, alpha=alpha),
        grid=(t // tm,),
        in_specs=[row, _full((hbf, D_FF)), _full((D_MODEL, D_FF)), _full((D_MODEL, D_FF)),
                  _full((CONV_F, D_FF)), _full((1, D_FF)), _full((D_FF, D_MODEL)), _full((1, D_MODEL)),
                  _full((1, D_MODEL))],
        out_specs=[row, _full((hbf, D_FF))],
        out_shape=[jax.ShapeDtypeStruct((t, D_MODEL), F32), jax.ShapeDtypeStruct((hbf, D_FF), F32)],
        scratch_shapes=[pltpu.VMEM((hbf, D_FF), F32), pltpu.VMEM((hbf + tm, FF_CHUNK), F32),
                        pltpu.VMEM((tm, D_MODEL), F32)],
        compiler_params=_cparams(("arbitrary",)),
        name="ffn_ln3",
    )(x, hist_f, lw['w_gate'], lw['w_up'], lw['convf_w'], lw['convf_b'], lw['w_down'], lw['ln3_g'], lw['ln3_b'])


def _mem_proj_kernel(m_ref, wk_ref, wv_ref, k_ref, v_ref):
    mb = m_ref[...].astype(BF16)
    k_ref[...] = _dot(mb, wk_ref[...])
    v_ref[...] = _dot(mb, wv_ref[...])


def _mem_proj(mem, wk_bf, wv_bf):
    n = mem.shape[0]
    sd = jax.ShapeDtypeStruct((n, D_MODEL), F32)
    return pl.pallas_call(
        _mem_proj_kernel, out_shape=[sd, sd],
        compiler_params=pltpu.CompilerParams(vmem_limit_bytes=V7X_VMEM_LIMIT),
        name="mem_proj",
    )(mem, wk_bf, wv_bf)


def _pad_front(rows, n, width):
    return jnp.concatenate([jnp.zeros((n - rows.shape[0], width), F32), rows], axis=0)


def kernel(x_prompt, x_sample, mem_prompt, cache_k, cache_v, page_table, cache_mem_k, cache_mem_v, state_conv_b, state_conv_c, state_conv_f, rel_bias, w_in, w_out, convb_w, convb_b, lnb_g, lnb_b, convc_w, ln1_g, ln1_b, w_qx, w_kx, w_vx, w_ox, ln2_g, ln2_b, w_gate, w_up, convf_w, convf_b, w_down, ln3_g, ln3_b):
    bp, seq, _ = x_prompt.shape
    bs, ds, _ = x_sample.shape
    depth = w_in.shape[0]
    n_mem = mem_prompt.shape[1]
    alpha = _alpha(depth)
    assert bp == 1 and seq % MOBA_BLOCK == 0

    xp = x_prompt.reshape(seq, D_MODEL)
    xs = x_sample.transpose(1, 0, 2).reshape(ds * bs, D_MODEL)
    mem_p = mem_prompt.reshape(n_mem, D_MODEL)
    zeros_b = jnp.zeros((32, W_B), F32)
    zeros_c = jnp.zeros((8, W_C), F32)
    zeros_f = jnp.zeros((8, D_FF), F32)

    outs = {n: [] for n in ('kp', 'vp', 'ks', 'vs', 'mkp', 'mvp', 'cbp', 'cbs', 'ccp', 'ccs', 'cfp', 'cfs')}
    for l in range(depth):
        vec = lambda a: a[l].reshape(1, -1)
        lw = {'w_out': w_out[l].astype(BF16), 'convb_w': convb_w[l], 'convb_b': vec(convb_b),
              'lnb_g': vec(lnb_g), 'lnb_b': vec(lnb_b), 'convc_w': convc_w[l], 'ln1_g': vec(ln1_g),
              'ln1_b': vec(ln1_b), 'w_qx': w_qx[l].astype(BF16), 'w_ox': w_ox[l].astype(BF16),
              'ln2_g': vec(ln2_g), 'ln2_b': vec(ln2_b), 'w_gate': w_gate[l].astype(BF16),
              'w_up': w_up[l].astype(BF16), 'convf_w': convf_w[l], 'convf_b': vec(convf_b),
              'w_down': w_down[l].astype(BF16), 'ln3_g': vec(ln3_g), 'ln3_b': vec(ln3_b)}
        w_in_bf = w_in[l].astype(BF16)

        mk, mv = _mem_proj(mem_p, w_kx[l].astype(BF16), w_vx[l].astype(BF16))
        qt, kb, vt, k, v, glu, uc, go, kmean = _in_proj(xp, w_in_bf, prompt=True)
        ya = _moba_prompt(rel_bias, qt, kb, vt, kmean)
        x1 = _mix(xp, ya, glu, uc, go, zeros_b, zeros_c, lw, shift=1, alpha=alpha)
        x2 = _cross(x1, mk[None], mv[None], lw, alpha=alpha)
        xp, st_f = _ffn(x2, zeros_f, lw, shift=1, alpha=alpha)
        outs['kp'].append(k.reshape(1, seq, H_A, HD_A))
        outs['vp'].append(v.reshape(1, seq, H_A, HD_A))
        outs['mkp'].append(mk.reshape(1, n_mem, H_X, HD_X))
        outs['mvp'].append(mv.reshape(1, n_mem, H_X, HD_X))
        outs['cbp'].append(glu[None, seq - (CONV_B - 1):])
        outs['ccp'].append(uc[None, seq - (CONV_C - 1):])
        outs['cfp'].append(st_f[None, st_f.shape[0] - (CONV_F - 1):])

        hb = state_conv_b[l].transpose(1, 0, 2)
        hc = state_conv_c[l].transpose(1, 0, 2)
        hf = state_conv_f[l].transpose(1, 0, 2)
        q, k, v, glu, uc, go = _in_proj(xs, w_in_bf, prompt=False)
        seq_major = lambda a: a.reshape(ds, bs, -1).transpose(1, 0, 2)
        k_sm, v_sm = seq_major(k), seq_major(v)
        ya = _moba_sample(l, rel_bias, page_table, cache_k, cache_v, seq_major(q), k_sm, v_sm)
        ya = ya.transpose(1, 0, 2).reshape(ds * bs, W_A)
        x1 = _mix(xs, ya, glu, uc, go, hb.reshape(-1, W_B), hc.reshape(-1, W_C), lw, shift=bs, alpha=alpha)
        x2 = _cross(x1, cache_mem_k[l].reshape(bs, n_mem, D_MODEL), cache_mem_v[l].reshape(bs, n_mem, D_MODEL),
                    lw, alpha=alpha)
        xs, st_f = _ffn(x2, hf.reshape(-1, D_FF), lw, shift=bs, alpha=alpha)
        outs['ks'].append(k_sm.reshape(bs, ds, H_A, HD_A))
        outs['vs'].append(v_sm.reshape(bs, ds, H_A, HD_A))
        tail = lambda hist, new, n: jnp.concatenate([hist, new.reshape(ds, bs, -1)], axis=0)[-n:].transpose(1, 0, 2)
        outs['cbs'].append(tail(hb, glu, CONV_B - 1))
        outs['ccs'].append(tail(hc, uc, CONV_C - 1))
        outs['cfs'].append(st_f.reshape(CONV_F - 1, bs, D_FF).transpose(1, 0, 2))

    y_prompt = xp.reshape(1, seq, D_MODEL)
    y_sample = xs.reshape(ds, bs, D_MODEL).transpose(1, 0, 2)
    st = lambda n, axis: jnp.stack(outs[n], axis=axis)
    return (y_prompt, y_sample, st('kp', 1), st('vp', 1), st('ks', 1), st('vs', 1), st('mkp', 0), st('mvp', 0),
            st('cbp', 0), st('cbs', 0), st('ccp', 0), st('ccs', 0), st('cfp', 0), st('cfs', 0))
```

```python
import functools
import math

import numpy as np
import jax
import jax.numpy as jnp
from jax import lax
from jax.experimental import pallas as pl
from jax.experimental.pallas import tpu as pltpu

F32 = jnp.float32
BF16 = jnp.bfloat16

D_MODEL = 1024
PAGE_SIZE = 128
HD_A = 64
W_A = 512
H_A = 8
W_B = 256
W_C = 256
PROJ_IN = 2816
MOBA_BLOCK = 256
MOBA_TOPK = 3
CONV_B = 31
CONV_C = 3
CONV_F = 3
D_FF = 2816
FF_CHUNK = 256
H_X = 4
HD_X = 256
REL_BUCKETS = 32
REL_MAX_DIST = 128
LN_EPS = 1e-5

NEG = -1e30
VROWS = 80
FAR_UNROLL = 4
V7X_VMEM_LIMIT = 56 * 1024 * 1024


def _alpha(depth):
    return (2 * depth) ** 0.25


def _cparams(sem):
    return pltpu.CompilerParams(dimension_semantics=sem, vmem_limit_bytes=V7X_VMEM_LIMIT)


def _full(shape):
    return pl.BlockSpec(shape, lambda *_: (0,) * len(shape))


def _sigmoid(x):
    return 1.0 / (1.0 + jnp.exp(-x))


def _layer_norm(h, g, b):
    mu = jnp.mean(h, axis=-1, keepdims=True)
    d = h - mu
    var = jnp.mean(d * d, axis=-1, keepdims=True)
    return d * lax.rsqrt(var + LN_EPS) * g + b


def _dot(a, b):
    return jnp.dot(a, b, preferred_element_type=F32)


def _dot_nt(a, b):
    return lax.dot_general(a, b, (((1,), (1,)), ((), ())), preferred_element_type=F32)


def _rel_bucket_np(dist):
    n = np.maximum(dist, 0)
    max_exact = REL_BUCKETS // 2
    nf = np.maximum(n, max_exact).astype(np.float32)
    large = max_exact + (np.log(nf / np.float32(max_exact)) / np.float32(math.log(REL_MAX_DIST / max_exact))
                         * np.float32(REL_BUCKETS - max_exact)).astype(np.int32)
    return np.where(n < max_exact, n, np.minimum(large, REL_BUCKETS - 1)).astype(np.int32)


def _in_proj_kernel(x_ref, w_ref, *outs, prompt, tm):
    xb = x_ref[...].astype(BF16)

    def proj(lo, hi):
        return _dot(xb, w_ref[:, lo:hi])

    q = proj(0, W_A) * (HD_A ** -0.5)
    k = proj(W_A, 2 * W_A)
    v = proj(2 * W_A, 3 * W_A)
    o = 3 * W_A
    glu = proj(o, o + W_B) * _sigmoid(proj(o + W_B, o + 2 * W_B))
    o += 2 * W_B
    gate_out = proj(o, o + W_C)
    uc = proj(o + W_C, o + 2 * W_C) * proj(o + 2 * W_C, o + 3 * W_C)
    if prompt:
        qt_ref, kb_ref, vt_ref, k_ref, v_ref, glu_ref, uc_ref, go_ref, km_ref = outs
        for c in range(W_A // 128):
            qt_ref[c * 128:(c + 1) * 128, :] = q[:, c * 128:(c + 1) * 128].T.astype(BF16)
        kb_ref[...] = k.astype(BF16)
        for s in range(tm // MOBA_BLOCK):
            rows = slice(s * MOBA_BLOCK, (s + 1) * MOBA_BLOCK)
            for c in range(W_A // 128):
                vt2 = v[rows, c * 128:(c + 1) * 128].T.astype(BF16)
                for hh in range(2):
                    r0 = (2 * c + hh) * VROWS
                    vt_ref[s, r0:r0 + HD_A, :] = vt2[hh * HD_A:(hh + 1) * HD_A, :]
                    vt_ref[s, r0 + HD_A:r0 + VROWS, :] = jnp.ones((VROWS - HD_A, MOBA_BLOCK), BF16)
            km_ref[s] = jnp.sum(k[rows, :], axis=0, keepdims=True) * (1.0 / MOBA_BLOCK)
    else:
        q_ref, k_ref, v_ref, glu_ref, uc_ref, go_ref = outs
        q_ref[...] = q
    k_ref[...] = k
    v_ref[...] = v
    glu_ref[...] = glu
    uc_ref[...] = uc
    go_ref[...] = gate_out


def _in_proj(x, w_bf, *, prompt):
    t = x.shape[0]
    tm = min(512, t)
    assert t % tm == 0
    nb = t // MOBA_BLOCK
    row = lambda n: pl.BlockSpec((tm, n), lambda i: (i, 0))
    sd = jax.ShapeDtypeStruct
    if prompt:
        assert tm % MOBA_BLOCK == 0
        spb = tm // MOBA_BLOCK
        out_shape = [sd((W_A, t), BF16), sd((t, W_A), BF16), sd((nb, H_A * VROWS, MOBA_BLOCK), BF16),
                     sd((t, W_A), F32), sd((t, W_A), F32), sd((t, W_B), F32), sd((t, W_C), F32),
                     sd((t, W_C), F32), sd((nb, 1, W_A), F32)]
        out_specs = [pl.BlockSpec((W_A, tm), lambda i: (0, i)), row(W_A),
                     pl.BlockSpec((spb, H_A * VROWS, MOBA_BLOCK), lambda i: (i, 0, 0)),
                     row(W_A), row(W_A), row(W_B), row(W_C), row(W_C),
                     pl.BlockSpec((spb, 1, W_A), lambda i: (i, 0, 0))]
    else:
        out_shape = [sd((t, W_A), F32), sd((t, W_A), F32), sd((t, W_A), F32), sd((t, W_B), F32),
                     sd((t, W_C), F32), sd((t, W_C), F32)]
        out_specs = [row(W_A), row(W_A), row(W_A), row(W_B), row(W_C), row(W_C)]
    return pl.pallas_call(
        functools.partial(_in_proj_kernel, prompt=prompt, tm=tm),
        grid=(t // tm,),
        in_specs=[row(D_MODEL), _full((D_MODEL, PROJ_IN))],
        out_specs=out_specs, out_shape=out_shape,
        compiler_params=_cparams(("arbitrary",)),
        name="in_proj_prompt" if prompt else "in_proj_sample",
    )(x, w_bf)


def _moba_prompt_kernel(rb_ref, bko_ref, bkp_ref, qt_ref, k_ref, vt_ref, km_ref, o_ref,
                        bown, bprev, sel_ref, m_ref, acc_ref, sping, spong, *, nb):
    p = pl.program_id(0)
    i = pl.program_id(1)
    blk = MOBA_BLOCK

    @pl.when(i == 0)
    def _():
        bko = bko_ref[...]
        bkp = bkp_ref[...]
        for hh in range(2):
            h = 2 * p + hh
            far = rb_ref[REL_BUCKETS - 1, h]
            bo = jnp.full((blk, blk), NEG, F32)
            bp = jnp.zeros((blk, blk), F32)
            for b in range(REL_BUCKETS):
                val = rb_ref[b, h] - far
                bo = jnp.where(bko == b, val, bo)
                bp = jnp.where(bkp == b, val, bp)
            bown[hh] = bo
            bprev[hh] = bp

    qt = qt_ref[...]
    rowi = lax.broadcasted_iota(jnp.int32, qt.shape, 0)
    zero = jnp.zeros_like(qt)
    qts = [jnp.where(rowi < HD_A, qt, zero), jnp.where(rowi >= HD_A, qt, zero)]

    km = km_ref[...].astype(BF16)
    bidx = lax.broadcasted_iota(jnp.int32, (nb, blk), 0)
    for hh in range(2):
        g = _dot(km, qts[hh])
        g = jnp.where(bidx < i, g, -jnp.inf)
        selected = jnp.zeros((nb, blk), jnp.bool_)
        for _ in range(MOBA_TOPK):
            mx = jnp.max(g, axis=0, keepdims=True)
            first = jnp.min(jnp.where(g == mx, bidx, nb), axis=0, keepdims=True)
            hit = (bidx == first) & (mx > -jnp.inf)
            selected = selected | hit
            g = jnp.where(hit, -jnp.inf, g)
        sel_ref[hh] = jnp.where(selected, 0.0, NEG)
        m_ref[hh] = jnp.full((1, blk), NEG, F32)
        acc_ref[hh] = jnp.zeros((VROWS, blk), F32)

    def scores(j, hh):
        return _dot(k_ref[j], qts[hh])

    def softmax_pv(s, j, hh, colbias):
        mx = jnp.max(s, axis=0, keepdims=True)
        if colbias is not None:
            mx = mx + colbias
        m_old = m_ref[hh]
        m_new = jnp.maximum(m_old, mx)
        a = jnp.exp(m_old - m_new)
        c = m_new if colbias is None else m_new - colbias
        pt = jnp.exp(s - c).astype(BF16)
        r = _dot(vt_ref[j, hh * VROWS:(hh + 1) * VROWS, :], pt)
        acc_ref[hh] = a * acc_ref[hh] + r
        m_ref[hh] = m_new

    n_far = jnp.maximum(i - 1, 0)
    jp = jnp.maximum(i - 1, 0)
    s_own = [scores(i, hh) + bown[hh] for hh in range(2)]
    s_prev = [scores(jp, hh) + bprev[hh] for hh in range(2)]
    for hh in range(2):
        sping[hh] = scores(0, hh)
    for hh in range(2):
        softmax_pv(s_own[hh], i, hh, None)
    for hh in range(2):
        prev_mask = jnp.where(i >= 1, sel_ref[hh, pl.ds(jp, 1), :], NEG)
        softmax_pv(s_prev[hh], jp, hh, prev_mask)

    bufs = [sping, spong]

    def far_blocks(t, carry):
        last = n_far - 1
        for u in range(FAR_UNROLL):
            cur, nxt = bufs[u % 2], bufs[1 - u % 2]
            j = t * FAR_UNROLL + u
            jn = jnp.minimum(j + 1, last)
            for hh in range(2):
                nxt[hh] = scores(jn, hh)
            jc = jnp.minimum(j, last)
            for hh in range(2):
                mask = jnp.where(j <= last, sel_ref[hh, pl.ds(jc, 1), :], NEG)
                softmax_pv(cur[hh], jc, hh, mask)
        return carry

    lax.fori_loop(0, (n_far + FAR_UNROLL - 1) // FAR_UNROLL, far_blocks, 0)

    outs = []
    for hh in range(2):
        acc = acc_ref[hh]
        outs.append(acc[0:HD_A, :] / acc[HD_A:HD_A + 1, :])
    o_ref[...] = jnp.concatenate(outs, axis=0).T


def _moba_prompt(rel_bias, qt, kb, vt, kmean):
    t = qt.shape[1]
    nb = t // MOBA_BLOCK
    blk = MOBA_BLOCK
    c = np.arange(blk)[:, None]
    r = np.arange(blk)[None, :]
    bk_own = np.where(c <= r, _rel_bucket_np(r - c), -1).astype(np.int32)
    bk_prev = _rel_bucket_np(blk + r - c)
    smem = pl.BlockSpec(memory_space=pltpu.SMEM)
    return pl.pallas_call(
        functools.partial(_moba_prompt_kernel, nb=nb),
        grid=(H_A // 2, nb),
        in_specs=[smem, _full((blk, blk)), _full((blk, blk)),
                  pl.BlockSpec((128, blk), lambda p, i: (p, i)),
                  pl.BlockSpec((nb, blk, 128), lambda p, i: (0, 0, p)),
                  pl.BlockSpec((nb, 2 * VROWS, blk), lambda p, i: (0, p, 0)),
                  pl.BlockSpec((nb, 128), lambda p, i: (0, p))],
        out_specs=pl.BlockSpec((blk, 128), lambda p, i: (i, p)),
        out_shape=jax.ShapeDtypeStruct((t, W_A), F32),
        scratch_shapes=[pltpu.VMEM((2, blk, blk), F32), pltpu.VMEM((2, blk, blk), F32),
                        pltpu.VMEM((2, nb, blk), F32), pltpu.VMEM((2, 1, blk), F32),
                        pltpu.VMEM((2, VROWS, blk), F32),
                        pltpu.VMEM((2, blk, blk), F32), pltpu.VMEM((2, blk, blk), F32)],
        compiler_params=_cparams(("arbitrary", "arbitrary")),
        name="moba_prompt",
    )(rel_bias, jnp.asarray(bk_own), jnp.asarray(bk_prev), qt, kb.reshape(nb, blk, W_A), vt,
      kmean.reshape(nb, W_A))


def _head_rows(x, n_tok):
    lane_head = lax.broadcasted_iota(jnp.int32, (H_A, W_A), 1) // HD_A
    sub = lax.broadcasted_iota(jnp.int32, (H_A, W_A), 0)
    parts = [jnp.where(lane_head == sub, jnp.broadcast_to(x[t:t + 1, :], (H_A, W_A)), 0.0)
             for t in range(n_tok)]
    return jnp.concatenate(parts, axis=0)


def _moba_sample_k_kernel(pt_ref, q_ref, kn_ref, rbr_ref, bkl_ref, *rest, u, nchunk, ds):
    kpages = rest[:u]
    p_ref, st_ref = rest[u], rest[u + 1]
    qbd_ref, s_ref, ks_ref = rest[u + 2:]
    c = pl.program_id(1)
    nrow = ds * H_A
    bpc = u // 2
    nbp = nchunk * bpc

    @pl.when(c == 0)
    def _():
        qbd_ref[...] = _head_rows(q_ref[...], ds).astype(BF16)
        ks_ref[...] = jnp.zeros_like(ks_ref)

    qbd = qbd_ref[...]
    for j in range(u):
        s_ref[c, :, j * PAGE_SIZE:(j + 1) * PAGE_SIZE] = _dot(qbd, kpages[j][...].astype(BF16))
    ks = ks_ref[...]
    lane_blk = lax.broadcasted_iota(jnp.int32, ks.shape, 1)
    for v in range(bpc):
        col = jnp.sum(kpages[2 * v][...] + kpages[2 * v + 1][...], axis=1, keepdims=True)
        ks = jnp.where(lane_blk == c * bpc + v, col, ks)
    ks_ref[...] = ks

    @pl.when(c == nchunk - 1)
    def _():
        blk = MOBA_BLOCK
        km = (ks_ref[...] * (1.0 / blk)).astype(BF16)
        g = _dot(qbd, km)
        lane = lax.broadcasted_iota(jnp.int32, g.shape, 1)
        g = jnp.where(lane < nbp, g, -jnp.inf)
        selected = jnp.zeros(g.shape, jnp.bool_)
        for _ in range(min(MOBA_TOPK, nbp)):
            mx = jnp.max(g, axis=1, keepdims=True)
            first = jnp.min(jnp.where(g == mx, lane, nbp), axis=1, keepdims=True)
            hit = (lane == first) & (mx > -jnp.inf)
            selected = selected | hit
            g = jnp.where(hit, -jnp.inf, g)
        maskadd = jnp.where(selected, 0.0, NEG)

        rbr = rbr_ref[...]
        far = rbr[:, REL_BUCKETS - 1:REL_BUCKETS]
        bkl = bkl_ref[...]
        bias_last = jnp.zeros((nrow, blk), F32)
        for b in range(REL_BUCKETS):
            bias_last = jnp.where(bkl == b, rbr[:, b:b + 1], bias_last)

        trow = lax.broadcasted_iota(jnp.int32, (nrow, 1), 0) // H_A
        qf = qbd.astype(F32)
        knb = kn_ref[...].astype(BF16).astype(F32)
        own = []
        for cc in range(ds):
            sc = jnp.sum(qf * knb[cc:cc + 1, :], axis=1, keepdims=True)
            bias = jnp.zeros((nrow, 1), F32)
            for d in range(ds - cc):
                bias = jnp.where(trow - cc == d, rbr[:, d:d + 1], bias)
            own.append(jnp.where(trow >= cc, sc + bias, NEG))

        def logits(n):
            cn, off = n // bpc, (n % bpc) * blk
            b = bias_last if n == nbp - 1 else far
            return s_ref[cn, :, off:off + blk] + b + maskadd[:, n:n + 1]

        mt = logits(0)
        for n in range(1, nbp):
            mt = jnp.maximum(mt, logits(n))
        m = jnp.max(mt, axis=1, keepdims=True)
        for cc in range(ds):
            m = jnp.maximum(m, own[cc])
        lt = jnp.zeros((nrow, blk), F32)
        for n in range(nbp):
            cn, off = n // bpc, (n % bpc) * blk
            pn = jnp.exp(logits(n) - m)
            lt = lt + pn
            p_ref[cn, :, off:off + blk] = pn.astype(BF16)
        l = jnp.sum(lt, axis=1, keepdims=True)
        lane128 = lax.broadcasted_iota(jnp.int32, (nrow, 128), 1)
        stats = jnp.zeros((nrow, 128), F32)
        for cc in range(ds):
            po = jnp.exp(own[cc] - m)
            l = l + po
            stats = jnp.where(lane128 == cc, po, stats)
        st_ref[...] = jnp.where(lane128 == ds, l, stats)


def _moba_sample_v_kernel(pt_ref, p_ref, st_ref, vn_ref, *rest, u, nchunk, ds):
    vpages = rest[:u]
    o_ref, acc_ref = rest[u], rest[u + 1]
    c = pl.program_id(1)

    @pl.when(c == 0)
    def _():
        acc_ref[...] = jnp.zeros_like(acc_ref)

    acc = acc_ref[...]
    for j in range(u):
        acc = acc + _dot_nt(p_ref[:, j * PAGE_SIZE:(j + 1) * PAGE_SIZE], vpages[j][...].astype(BF16))
    acc_ref[...] = acc

    @pl.when(c == nchunk - 1)
    def _():
        a = acc_ref[...]
        st = st_ref[...]
        vnb = vn_ref[...].astype(BF16).astype(F32)
        for cc in range(ds):
            a = a + st[:, cc:cc + 1].astype(BF16).astype(F32) * vnb[cc:cc + 1, :]
        a = a / st[:, ds:ds + 1]
        lane_head = lax.broadcasted_iota(jnp.int32, (H_A, W_A), 1) // HD_A
        sub = lax.broadcasted_iota(jnp.int32, (H_A, W_A), 0)
        for t in range(ds):
            rows = a[t * H_A:(t + 1) * H_A, :]
            o_ref[t:t + 1, :] = jnp.sum(jnp.where(lane_head == sub, rows, 0.0), axis=0, keepdims=True)


def _moba_sample(layer, rel_bias, page_table, cache_k, cache_v, q, k_new, v_new):
    b, ds, _ = q.shape
    n_pages = page_table.shape[1]
    past_len = n_pages * PAGE_SIZE
    assert past_len % MOBA_BLOCK == 0 and ds <= PAGE_SIZE - 1
    nbp = past_len // MOBA_BLOCK
    u = min(16, n_pages)
    assert n_pages % u == 0 and u % 2 == 0 and nbp <= 128
    nchunk = n_pages // u
    nrow = ds * H_A
    n_pool, depth = cache_k.shape[:2]
    ck = cache_k.transpose(0, 1, 3, 4, 2).reshape(n_pool, depth, W_A, PAGE_SIZE)
    cv = cache_v.transpose(0, 1, 3, 4, 2).reshape(n_pool, depth, W_A, PAGE_SIZE)
    rbr = jnp.tile(rel_bias.T, (ds, 1))
    trow = np.arange(nrow)[:, None] // H_A
    bk_last = _rel_bucket_np(MOBA_BLOCK + trow - np.arange(MOBA_BLOCK)[None, :])

    def page_spec(j):
        return pl.BlockSpec((None, None, W_A, PAGE_SIZE),
                            lambda bi, c, pt: (pt[bi, c * u + j], layer, 0, 0))

    per_seq = lambda r, n: pl.BlockSpec((None, r, n), lambda bi, c, pt: (bi, 0, 0))
    const = lambda r, n: pl.BlockSpec((r, n), lambda bi, c, pt: (0, 0))
    probs, stats = pl.pallas_call(
        functools.partial(_moba_sample_k_kernel, u=u, nchunk=nchunk, ds=ds),
        grid_spec=pltpu.PrefetchScalarGridSpec(
            num_scalar_prefetch=1, grid=(b, nchunk),
            in_specs=[per_seq(ds, W_A), per_seq(ds, W_A), const(nrow, REL_BUCKETS), const(nrow, MOBA_BLOCK)]
                     + [page_spec(j) for j in range(u)],
            out_specs=[pl.BlockSpec((None, nchunk, nrow, u * PAGE_SIZE), lambda bi, c, pt: (bi, 0, 0, 0)),
                       per_seq(nrow, 128)],
            scratch_shapes=[pltpu.VMEM((nrow, W_A), BF16), pltpu.VMEM((nchunk, nrow, u * PAGE_SIZE), F32),
                            pltpu.VMEM((W_A, 128), F32)]),
        out_shape=[jax.ShapeDtypeStruct((b, nchunk, nrow, u * PAGE_SIZE), BF16),
                   jax.ShapeDtypeStruct((b, nrow, 128), F32)],
        compiler_params=_cparams(("arbitrary", "arbitrary")),
        name="moba_sample_keys",
    )(page_table, q, k_new, rbr, jnp.asarray(bk_last), *([ck] * u))
    return pl.pallas_call(
        functools.partial(_moba_sample_v_kernel, u=u, nchunk=nchunk, ds=ds),
        grid_spec=pltpu.PrefetchScalarGridSpec(
            num_scalar_prefetch=1, grid=(b, nchunk),
            in_specs=[pl.BlockSpec((None, None, nrow, u * PAGE_SIZE), lambda bi, c, pt: (bi, c, 0, 0)),
                      per_seq(nrow, 128), per_seq(ds, W_A)] + [page_spec(j) for j in range(u)],
            out_specs=per_seq(ds, W_A),
            scratch_shapes=[pltpu.VMEM((nrow, W_A), F32)]),
        out_shape=jax.ShapeDtypeStruct((b, ds, W_A), F32),
        compiler_params=_cparams(("arbitrary", "arbitrary")),
        name="moba_sample_values",
    )(page_table, probs, stats, v_new, *([cv] * u))


def _mix_kernel(x_ref, ya_ref, glu_ref, uc_ref, go_ref, hb_ref, hc_ref, cbw_ref, cbb_ref, lbg_ref, lbb_ref,
                ccw_ref, wo_ref, g1_ref, b1_ref, o_ref, extb, extc, *, tm, shift, hbb, hbc, alpha):
    @pl.when(pl.program_id(0) == 0)
    def _():
        extb[0:hbb, :] = hb_ref[...]
        extc[0:hbc, :] = hc_ref[...]

    extb[hbb:hbb + tm, :] = glu_ref[...]
    extc[hbc:hbc + tm, :] = uc_ref[...]
    offb = hbb - (CONV_B - 1) * shift
    cb = jnp.broadcast_to(cbb_ref[...], (tm, W_B))
    for w in range(CONV_B):
        cb = cb + extb[offb + w * shift:offb + w * shift + tm, :] * cbw_ref[w:w + 1, :]
    yb = _layer_norm(cb, lbg_ref[...], lbb_ref[...])
    yb = yb * _sigmoid(yb)
    offc = hbc - (CONV_C - 1) * shift
    cc = extc[offc:offc + tm, :] * ccw_ref[0:1, :]
    for w in range(1, CONV_C):
        cc = cc + extc[offc + w * shift:offc + w * shift + tm, :] * ccw_ref[w:w + 1, :]
    yc = go_ref[...] * cc
    mix = (_dot(ya_ref[...].astype(BF16), wo_ref[0:W_A, :])
           + _dot(yb.astype(BF16), wo_ref[W_A:W_A + W_B, :])
           + _dot(yc.astype(BF16), wo_ref[W_A + W_B:W_A + W_B + W_C, :]))
    o_ref[...] = _layer_norm(alpha * x_ref[...] + mix, g1_ref[...], b1_ref[...])
    nb_rows = extb[tm:tm + hbb, :]
    nc_rows = extc[tm:tm + hbc, :]
    extb[0:hbb, :] = nb_rows
    extc[0:hbc, :] = nc_rows


def _mix(x, ya, glu, uc, go, hist_b, hist_c, lw, *, shift, alpha):
    t = x.shape[0]
    tm = min(512, t)
    assert t % tm == 0
    hbb, hbc = hist_b.shape[0], hist_c.shape[0]
    row = lambda n: pl.BlockSpec((tm, n), lambda i: (i, 0))
    return pl.pallas_call(
        functools.partial(_mix_kernel, tm=tm, shift=shift, hbb=hbb, hbc=hbc, alpha=alpha),
        grid=(t // tm,),
        in_specs=[row(D_MODEL), row(W_A), row(W_B), row(W_C), row(W_C), _full((hbb, W_B)), _full((hbc, W_C)),
                  _full((CONV_B, W_B)), _full((1, W_B)), _full((1, W_B)), _full((1, W_B)), _full((CONV_C, W_C)),
                  _full((D_MODEL, D_MODEL)), _full((1, D_MODEL)), _full((1, D_MODEL))],
        out_specs=row(D_MODEL),
        out_shape=jax.ShapeDtypeStruct((t, D_MODEL), F32),
        scratch_shapes=[pltpu.VMEM((hbb + tm, W_B), F32), pltpu.VMEM((hbc + tm, W_C), F32)],
        compiler_params=_cparams(("arbitrary",)),
        name="mix_ln1",
    )(x, ya, glu, uc, go, hist_b, hist_c, lw['convb_w'], lw['convb_b'], lw['lnb_g'], lw['lnb_b'],
      lw['convc_w'], lw['w_out'], lw['ln1_g'], lw['ln1_b'])


def _cross_kernel(x_ref, mk_ref, mv_ref, wq_ref, wo_ref, g_ref, b_ref, o_ref, qx_ref, acc_ref,
                  *, tm, n_groups, alpha):
    g = pl.program_id(1)

    @pl.when(g == 0)
    def _():
        qx_ref[...] = (_dot(x_ref[...].astype(BF16), wq_ref[...]) * (HD_X ** -0.5)).astype(BF16)
        acc_ref[...] = jnp.zeros_like(acc_ref)

    mk = mk_ref[...].astype(BF16)
    mv = mv_ref[...].astype(BF16)
    heads = []
    for h in range(H_X):
        cols = slice(h * HD_X, (h + 1) * HD_X)
        s = _dot_nt(qx_ref[:, cols], mk[:, cols])
        s = s - jnp.max(s, axis=-1, keepdims=True)
        e = jnp.exp(s)
        pr = e / jnp.sum(e, axis=-1, keepdims=True)
        heads.append(_dot(pr.astype(BF16), mv[:, cols]))
    o = jnp.concatenate(heads, axis=-1)
    if n_groups == 1:
        acc_ref[...] = o
    else:
        rgrp = (pl.program_id(0) * tm + lax.broadcasted_iota(jnp.int32, (tm, 1), 0)) % n_groups
        acc_ref[...] = jnp.where(rgrp == g, o, acc_ref[...])

    @pl.when(g == n_groups - 1)
    def _():
        ox = _dot(acc_ref[...].astype(BF16), wo_ref[...])
        o_ref[...] = _layer_norm(alpha * x_ref[...] + ox, g_ref[...], b_ref[...])


def _cross(x, mem_k, mem_v, lw, *, alpha):
    t = x.shape[0]
    n_groups, n_mem, _ = mem_k.shape
    tm = min(512, t)
    assert t % tm == 0
    row = pl.BlockSpec((tm, D_MODEL), lambda i, g: (i, 0))
    mem = pl.BlockSpec((None, n_mem, D_MODEL), lambda i, g: (g, 0, 0))
    cst = lambda r, n: pl.BlockSpec((r, n), lambda i, g: (0, 0))
    return pl.pallas_call(
        functools.partial(_cross_kernel, tm=tm, n_groups=n_groups, alpha=alpha),
        grid=(t // tm, n_groups),
        in_specs=[row, mem, mem, cst(D_MODEL, D_MODEL), cst(D_MODEL, D_MODEL), cst(1, D_MODEL), cst(1, D_MODEL)],
        out_specs=row,
        out_shape=jax.ShapeDtypeStruct((t, D_MODEL), F32),
        scratch_shapes=[pltpu.VMEM((tm, D_MODEL), BF16), pltpu.VMEM((tm, D_MODEL), F32)],
        compiler_params=_cparams(("arbitrary", "arbitrary")),
        name="cross_ln2",
    )(x, mem_k, mem_v, lw['w_qx'], lw['w_ox'], lw['ln2_g'], lw['ln2_b'])


def _ffn_kernel(x_ref, hf_ref, wg_ref, wu_ref, cfw_ref, cfb_ref, wd_ref, g_ref, b_ref, o_ref, st_ref,
                carry, ext, acc_ref, *, tm, shift, hbf, alpha):
    @pl.when(pl.program_id(0) == 0)
    def _():
        carry[...] = hf_ref[...]

    xb = x_ref[...].astype(BF16)
    off = hbf - (CONV_F - 1) * shift
    for c in range(D_FF // FF_CHUNK):
        cols = slice(c * FF_CHUNK, (c + 1) * FF_CHUNK)
        ext[0:hbf, :] = carry[:, cols]
        ext[hbf:hbf + tm, :] = _dot(xb, wg_ref[:, cols])
        gf = jnp.broadcast_to(cfb_ref[:, cols], (tm, FF_CHUNK))
        for w in range(CONV_F):
            gf = gf + ext[off + w * shift:off + w * shift + tm, :] * cfw_ref[w:w + 1, cols]
        carry[:, cols] = ext[tm:tm + hbf, :]
        hid = gf * _sigmoid(gf) * _dot(xb, wu_ref[:, cols])
        part = _dot(hid.astype(BF16), wd_ref[cols, :])
        if c == 0:
            acc_ref[...] = part
        else:
            acc_ref[...] += part
    o_ref[...] = _layer_norm(alpha * x_ref[...] + acc_ref[...], g_ref[...], b_ref[...])
    st_ref[...] = carry[...]


def _ffn(x, hist_f, lw, *, shift, alpha):
    t = x.shape[0]
    tm = min(512, t)
    assert t % tm == 0
    hbf = hist_f.shape[0]
    row = pl.BlockSpec((tm, D_MODEL), lambda i: (i, 0))
    return pl.pallas_call(
        functools.partial(_ffn_kernel, tm=tm, shift=shift, hbf=hbf, alpha=alpha),
        grid=(t // tm,),
        in_specs=[row, _full((hbf, D_FF)), _full((D_MODEL, D_FF)), _full((D_MODEL, D_FF)),
                  _full((CONV_F, D_FF)), _full((1, D_FF)), _full((D_FF, D_MODEL)), _full((1, D_MODEL)),
                  _full((1, D_MODEL))],
        out_specs=[row, _full((hbf, D_FF))],
        out_shape=[jax.ShapeDtypeStruct((t, D_MODEL), F32), jax.ShapeDtypeStruct((hbf, D_FF), F32)],
        scratch_shapes=[pltpu.VMEM((hbf, D_FF), F32), pltpu.VMEM((hbf + tm, FF_CHUNK), F32),
                        pltpu.VMEM((tm, D_MODEL), F32)],
        compiler_params=_cparams(("arbitrary",)),
        name="ffn_ln3",
    )(x, hist_f, lw['w_gate'], lw['w_up'], lw['convf_w'], lw['convf_b'], lw['w_down'], lw['ln3_g'], lw['ln3_b'])


def _mem_proj_kernel(m_ref, wk_ref, wv_ref, k_ref, v_ref):
    mb = m_ref[...].astype(BF16)
    k_ref[...] = _dot(mb, wk_ref[...])
    v_ref[...] = _dot(mb, wv_ref[...])


def _mem_proj(mem, wk_bf, wv_bf):
    n = mem.shape[0]
    sd = jax.ShapeDtypeStruct((n, D_MODEL), F32)
    return pl.pallas_call(
        _mem_proj_kernel, out_shape=[sd, sd],
        compiler_params=pltpu.CompilerParams(vmem_limit_bytes=V7X_VMEM_LIMIT),
        name="mem_proj",
    )(mem, wk_bf, wv_bf)


def _pad_front(rows, n, width):
    return jnp.concatenate([jnp.zeros((n - rows.shape[0], width), F32), rows], axis=0)


def kernel(x_prompt, x_sample, mem_prompt, cache_k, cache_v, page_table, cache_mem_k, cache_mem_v, state_conv_b, state_conv_c, state_conv_f, rel_bias, w_in, w_out, convb_w, convb_b, lnb_g, lnb_b, convc_w, ln1_g, ln1_b, w_qx, w_kx, w_vx, w_ox, ln2_g, ln2_b, w_gate, w_up, convf_w, convf_b, w_down, ln3_g, ln3_b):
    bp, seq, _ = x_prompt.shape
    bs, ds, _ = x_sample.shape
    depth = w_in.shape[0]
    n_mem = mem_prompt.shape[1]
    alpha = _alpha(depth)
    assert bp == 1 and seq % MOBA_BLOCK == 0

    xp = x_prompt.reshape(seq, D_MODEL)
    xs = x_sample.transpose(1, 0, 2).reshape(ds * bs, D_MODEL)
    mem_p = mem_prompt.reshape(n_mem, D_MODEL)
    zeros_b = jnp.zeros((32, W_B), F32)
    zeros_c = jnp.zeros((8, W_C), F32)
    zeros_f = jnp.zeros((8, D_FF), F32)

    outs = {n: [] for n in ('kp', 'vp', 'ks', 'vs', 'mkp', 'mvp', 'cbp', 'cbs', 'ccp', 'ccs', 'cfp', 'cfs')}
    for l in range(depth):
        vec = lambda a: a[l].reshape(1, -1)
        lw = {'w_out': w_out[l].astype(BF16), 'convb_w': convb_w[l], 'convb_b': vec(convb_b),
              'lnb_g': vec(lnb_g), 'lnb_b': vec(lnb_b), 'convc_w': convc_w[l], 'ln1_g': vec(ln1_g),
              'ln1_b': vec(ln1_b), 'w_qx': w_qx[l].astype(BF16), 'w_ox': w_ox[l].astype(BF16),
              'ln2_g': vec(ln2_g), 'ln2_b': vec(ln2_b), 'w_gate': w_gate[l].astype(BF16),
              'w_up': w_up[l].astype(BF16), 'convf_w': convf_w[l], 'convf_b': vec(convf_b),
              'w_down': w_down[l].astype(BF16), 'ln3_g': vec(ln3_g), 'ln3_b': vec(ln3_b)}
        w_in_bf = w_in[l].astype(BF16)

        mk, mv = _mem_proj(mem_p, w_kx[l].astype(BF16), w_vx[l].astype(BF16))
        qt, kb, vt, k, v, glu, uc, go, kmean = _in_proj(xp, w_in_bf, prompt=True)
        ya = _moba_prompt(rel_bias, qt, kb, vt, kmean)
        x1 = _mix(xp, ya, glu, uc, go, zeros_b, zeros_c, lw, shift=1, alpha=alpha)
        x2 = _cross(x1, mk[None], mv[None], lw, alpha=alpha)
        xp, st_f = _ffn(x2, zeros_f, lw, shift=1, alpha=alpha)
        outs['kp'].append(k.reshape(1, seq, H_A, HD_A))
        outs['vp'].append(v.reshape(1, seq, H_A, HD_A))
        outs['mkp'].append(mk.reshape(1, n_mem, H_X, HD_X))
        outs['mvp'].append(mv.reshape(1, n_mem, H_X, HD_X))
        outs['cbp'].append(glu[None, seq - (CONV_B - 1):])
        outs['ccp'].append(uc[None, seq - (CONV_C - 1):])
        outs['cfp'].append(st_f[None, st_f.shape[0] - (CONV_F - 1):])

        hb = state_conv_b[l].transpose(1, 0, 2)
        hc = state_conv_c[l].transpose(1, 0, 2)
        hf = state_conv_f[l].transpose(1, 0, 2)
        q, k, v, glu, uc, go = _in_proj(xs, w_in_bf, prompt=False)
        seq_major = lambda a: a.reshape(ds, bs, -1).transpose(1, 0, 2)
        k_sm, v_sm = seq_major(k), seq_major(v)
        ya = _moba_sample(l, rel_bias, page_table, cache_k, cache_v, seq_major(q), k_sm, v_sm)
        ya = ya.transpose(1, 0, 2).reshape(ds * bs, W_A)
        x1 = _mix(xs, ya, glu, uc, go, hb.reshape(-1, W_B), hc.reshape(-1, W_C), lw, shift=bs, alpha=alpha)
        x2 = _cross(x1, cache_mem_k[l].reshape(bs, n_mem, D_MODEL), cache_mem_v[l].reshape(bs, n_mem, D_MODEL),
                    lw, alpha=alpha)
        xs, st_f = _ffn(x2, hf.reshape(-1, D_FF), lw, shift=bs, alpha=alpha)
        outs['ks'].append(k_sm.reshape(bs, ds, H_A, HD_A))
        outs['vs'].append(v_sm.reshape(bs, ds, H_A, HD_A))
        tail = lambda hist, new, n: jnp.concatenate([hist, new.reshape(ds, bs, -1)], axis=0)[-n:].transpose(1, 0, 2)
        outs['cbs'].append(tail(hb, glu, CONV_B - 1))
        outs['ccs'].append(tail(hc, uc, CONV_C - 1))
        outs['cfs'].append(st_f.reshape(CONV_F - 1, bs, D_FF).transpose(1, 0, 2))

    y_prompt = xp.reshape(1, seq, D_MODEL)
    y_sample = xs.reshape(ds, bs, D_MODEL).transpose(1, 0, 2)
    st = lambda n, axis: jnp.stack(outs[n], axis=axis)
    return (y_prompt, y_sample, st('kp', 1), st('vp', 1), st('ks', 1), st('vs', 1), st('mkp', 0), st('mvp', 0),
            st('cbp', 0), st('cbs', 0), st('ccp', 0), st('ccs', 0), st('cfp', 0), st('cfs', 0))
```

```python
import functools
import math

import numpy as np
import jax
import jax.numpy as jnp
from jax import lax
from jax.experimental import pallas as pl
from jax.experimental.pallas import tpu as pltpu

F32 = jnp.float32
BF16 = jnp.bfloat16

D_MODEL = 1024
PAGE_SIZE = 128
HD_A = 64
W_A = 512
H_A = 8
W_B = 256
W_C = 256
PROJ_IN = 2816
MOBA_BLOCK = 256
MOBA_TOPK = 3
CONV_B = 31
CONV_C = 3
CONV_F = 3
D_FF = 2816
FF_CHUNK = 256
H_X = 4
HD_X = 256
REL_BUCKETS = 32
REL_MAX_DIST = 128
LN_EPS = 1e-5
SUBLANES = 8

NEG = -1e30
VROWS = 80
FAR_UNROLL = 8
LOG2E = math.log2(math.e)
V7X_VMEM_LIMIT = 56 * 1024 * 1024


def _alpha(depth):
    return (2 * depth) ** 0.25


def _cparams(sem):
    return pltpu.CompilerParams(dimension_semantics=sem, vmem_limit_bytes=V7X_VMEM_LIMIT)


def _full(shape):
    return pl.BlockSpec(shape, lambda *_: (0,) * len(shape))


def _sigmoid(x):
    return 1.0 / (1.0 + jnp.exp(-x))


def _layer_norm(h, g, b):
    mu = jnp.mean(h, axis=-1, keepdims=True)
    d = h - mu
    var = jnp.mean(d * d, axis=-1, keepdims=True)
    return d * lax.rsqrt(var + LN_EPS) * g + b


def _dot(a, b):
    return jnp.dot(a, b, preferred_element_type=F32)


def _dot_nt(a, b):
    return lax.dot_general(a, b, (((1,), (1,)), ((), ())), preferred_element_type=F32)


def _rel_bucket_np(dist):
    n = np.maximum(dist, 0)
    max_exact = REL_BUCKETS // 2
    nf = np.maximum(n, max_exact).astype(np.float32)
    large = max_exact + (np.log(nf / np.float32(max_exact)) / np.float32(math.log(REL_MAX_DIST / max_exact))
                         * np.float32(REL_BUCKETS - max_exact)).astype(np.int32)
    return np.where(n < max_exact, n, np.minimum(large, REL_BUCKETS - 1)).astype(np.int32)


def _in_proj_kernel(x_ref, w_ref, *outs, prompt, tm):
    xb = x_ref[...].astype(BF16)

    def proj(lo, hi):
        return _dot(xb, w_ref[:, lo:hi])

    q = proj(0, W_A) * (HD_A ** -0.5 * (LOG2E if prompt else 1.0))
    k = proj(W_A, 2 * W_A)
    v = proj(2 * W_A, 3 * W_A)
    o = 3 * W_A
    glu = proj(o, o + W_B) * _sigmoid(proj(o + W_B, o + 2 * W_B))
    o += 2 * W_B
    gate_out = proj(o, o + W_C)
    uc = proj(o + W_C, o + 2 * W_C) * proj(o + 2 * W_C, o + 3 * W_C)
    if prompt:
        qt_ref, kb_ref, vt_ref, k_ref, v_ref, glu_ref, uc_ref, go_ref, km_ref = outs
        for c in range(W_A // 128):
            cols = slice(c * 128, (c + 1) * 128)
            qt_ref[cols, :] = q[:, cols].T.astype(BF16)
            k_ref[cols, :] = k[:, cols].T
            v_ref[cols, :] = v[:, cols].T
        kb_ref[...] = k.astype(BF16)
        for s in range(tm // MOBA_BLOCK):
            rows = slice(s * MOBA_BLOCK, (s + 1) * MOBA_BLOCK)
            for c in range(W_A // 128):
                vt2 = v[rows, c * 128:(c + 1) * 128].T.astype(BF16)
                for hh in range(2):
                    r0 = (2 * c + hh) * VROWS
                    vt_ref[s, r0:r0 + HD_A, :] = vt2[hh * HD_A:(hh + 1) * HD_A, :]
                    vt_ref[s, r0 + HD_A:r0 + VROWS, :] = jnp.ones((VROWS - HD_A, MOBA_BLOCK), BF16)
            km_ref[s] = jnp.sum(k[rows, :], axis=0, keepdims=True) * (1.0 / MOBA_BLOCK)
    else:
        q_ref, k_ref, v_ref, glu_ref, uc_ref, go_ref = outs
        q_ref[...] = q
        k_ref[...] = k
        v_ref[...] = v
    glu_ref[...] = glu
    uc_ref[...] = uc
    go_ref[...] = gate_out


def _in_proj(x, w_bf, *, prompt):
    t = x.shape[0]
    tm = min(512, t)
    assert t % tm == 0
    nb = t // MOBA_BLOCK
    row = lambda n: pl.BlockSpec((tm, n), lambda i: (i, 0))
    sd = jax.ShapeDtypeStruct
    if prompt:
        assert tm % MOBA_BLOCK == 0
        spb = tm // MOBA_BLOCK
        out_shape = [sd((W_A, t), BF16), sd((t, W_A), BF16), sd((nb, H_A * VROWS, MOBA_BLOCK), BF16),
                     sd((W_A, t), F32), sd((W_A, t), F32), sd((t, W_B), F32), sd((t, W_C), F32),
                     sd((t, W_C), F32), sd((nb, 1, W_A), F32)]
        col = pl.BlockSpec((W_A, tm), lambda i: (0, i))
        out_specs = [col, row(W_A),
                     pl.BlockSpec((spb, H_A * VROWS, MOBA_BLOCK), lambda i: (i, 0, 0)),
                     col, col, row(W_B), row(W_C), row(W_C),
                     pl.BlockSpec((spb, 1, W_A), lambda i: (i, 0, 0))]
    else:
        out_shape = [sd((t, W_A), F32), sd((t, W_A), F32), sd((t, W_A), F32), sd((t, W_B), F32),
                     sd((t, W_C), F32), sd((t, W_C), F32)]
        out_specs = [row(W_A), row(W_A), row(W_A), row(W_B), row(W_C), row(W_C)]
    return pl.pallas_call(
        functools.partial(_in_proj_kernel, prompt=prompt, tm=tm),
        grid=(t // tm,),
        in_specs=[row(D_MODEL), _full((D_MODEL, PROJ_IN))],
        out_specs=out_specs, out_shape=out_shape,
        compiler_params=_cparams(("arbitrary",)),
        name="in_proj_prompt" if prompt else "in_proj_sample",
    )(x, w_bf)


def _moba_prompt_kernel(rb_ref, bko_ref, bkp_ref, qt_ref, k_ref, vt_ref, km_ref, o_ref,
                        bown, bprev, sel_ref, m_ref, acc_ref, sping, spong, *, nb):
    p = pl.program_id(0)
    i = pl.program_id(1)
    blk = MOBA_BLOCK

    @pl.when(i == 0)
    def _():
        bko = bko_ref[...]
        bkp = bkp_ref[...]
        for hh in range(2):
            h = 2 * p + hh
            far = rb_ref[REL_BUCKETS - 1, h]
            bo = jnp.full((blk, blk), NEG, F32)
            bp = jnp.zeros((blk, blk), F32)
            for b in range(REL_BUCKETS):
                val = (rb_ref[b, h] - far) * LOG2E
                bo = jnp.where(bko == b, val, bo)
                bp = jnp.where(bkp == b, val, bp)
            bown[hh] = bo
            bprev[hh] = bp

    qt = qt_ref[...]
    rowi = lax.broadcasted_iota(jnp.int32, qt.shape, 0)
    zero = jnp.zeros_like(qt)
    qts = [jnp.where(rowi < HD_A, qt, zero), jnp.where(rowi >= HD_A, qt, zero)]

    km = km_ref[...].astype(BF16)
    bidx = lax.broadcasted_iota(jnp.int32, (nb, blk), 0)
    for hh in range(2):
        g = _dot(km, qts[hh])
        g = jnp.where(bidx < i, g, -jnp.inf)
        selected = jnp.zeros((nb, blk), jnp.bool_)
        for _ in range(MOBA_TOPK):
            mx = jnp.max(g, axis=0, keepdims=True)
            first = jnp.min(jnp.where(g == mx, bidx, nb), axis=0, keepdims=True)
            hit = (bidx == first) & (mx > -jnp.inf)
            selected = selected | hit
            g = jnp.where(hit, -jnp.inf, g)
        sel_ref[hh] = jnp.where(selected, 0.0, NEG)
        m_ref[hh] = jnp.full((1, blk), NEG, F32)
        acc_ref[hh] = jnp.zeros((VROWS, blk), F32)

    def scores(j, hh):
        return _dot(k_ref[j], qts[hh])

    def softmax_pv(s, j, hh, colbias):
        mx = jnp.max(s, axis=0, keepdims=True)
        if colbias is not None:
            mx = mx + colbias
        m_old = m_ref[hh]
        m_new = jnp.maximum(m_old, mx)
        a = jnp.exp2(m_old - m_new)
        c = m_new if colbias is None else m_new - colbias
        pt = jnp.exp2(s - c).astype(BF16)
        r = _dot(vt_ref[j, hh * VROWS:(hh + 1) * VROWS, :], pt)
        acc_ref[hh] = a * acc_ref[hh] + r
        m_ref[hh] = m_new

    n_far = jnp.maximum(i - 1, 0)
    jp = jnp.maximum(i - 1, 0)
    s_own = [scores(i, hh) + bown[hh] for hh in range(2)]
    s_prev = [scores(jp, hh) + bprev[hh] for hh in range(2)]
    for hh in range(2):
        sping[hh] = scores(0, hh)
    for hh in range(2):
        softmax_pv(s_own[hh], i, hh, None)
    for hh in range(2):
        prev_mask = jnp.where(i >= 1, sel_ref[hh, pl.ds(jp, 1), :], NEG)
        softmax_pv(s_prev[hh], jp, hh, prev_mask)

    bufs = [sping, spong]

    def far_blocks(t, carry):
        last = n_far - 1
        for u in range(FAR_UNROLL):
            cur, nxt = bufs[u % 2], bufs[1 - u % 2]
            j = t * FAR_UNROLL + u
            jn = jnp.minimum(j + 1, last)
            for hh in range(2):
                nxt[hh] = scores(jn, hh)
            jc = jnp.minimum(j, last)
            for hh in range(2):
                mask = jnp.where(j <= last, sel_ref[hh, pl.ds(jc, 1), :], NEG)
                softmax_pv(cur[hh], jc, hh, mask)
        return carry

    lax.fori_loop(0, (n_far + FAR_UNROLL - 1) // FAR_UNROLL, far_blocks, 0)

    outs = []
    for hh in range(2):
        acc = acc_ref[hh]
        outs.append(acc[0:HD_A, :] / acc[HD_A:HD_A + 1, :])
    o_ref[...] = jnp.concatenate(outs, axis=0).T


def _moba_prompt(rel_bias, qt, kb, vt, kmean):
    t = qt.shape[1]
    nb = t // MOBA_BLOCK
    blk = MOBA_BLOCK
    c = np.arange(blk)[:, None]
    r = np.arange(blk)[None, :]
    bk_own = np.where(c <= r, _rel_bucket_np(r - c), -1).astype(np.int32)
    bk_prev = _rel_bucket_np(blk + r - c)
    smem = pl.BlockSpec(memory_space=pltpu.SMEM)
    return pl.pallas_call(
        functools.partial(_moba_prompt_kernel, nb=nb),
        grid=(H_A // 2, nb),
        in_specs=[smem, _full((blk, blk)), _full((blk, blk)),
                  pl.BlockSpec((128, blk), lambda p, i: (p, i)),
                  pl.BlockSpec((nb, blk, 128), lambda p, i: (0, 0, p)),
                  pl.BlockSpec((nb, 2 * VROWS, blk), lambda p, i: (0, p, 0)),
                  pl.BlockSpec((nb, 128), lambda p, i: (0, p))],
        out_specs=pl.BlockSpec((blk, 128), lambda p, i: (i, p)),
        out_shape=jax.ShapeDtypeStruct((t, W_A), F32),
        scratch_shapes=[pltpu.VMEM((2, blk, blk), F32), pltpu.VMEM((2, blk, blk), F32),
                        pltpu.VMEM((2, nb, blk), F32), pltpu.VMEM((2, 1, blk), F32),
                        pltpu.VMEM((2, VROWS, blk), F32),
                        pltpu.VMEM((2, blk, blk), F32), pltpu.VMEM((2, blk, blk), F32)],
        compiler_params=_cparams(("arbitrary", "arbitrary")),
        name="moba_prompt",
    )(rel_bias, jnp.asarray(bk_own), jnp.asarray(bk_prev), qt, kb.reshape(nb, blk, W_A), vt,
      kmean.reshape(nb, W_A))


def _head_rows(x, n_tok):
    lane_head = lax.broadcasted_iota(jnp.int32, (H_A, W_A), 1) // HD_A
    sub = lax.broadcasted_iota(jnp.int32, (H_A, W_A), 0)
    parts = [jnp.where(lane_head == sub, jnp.broadcast_to(x[t:t + 1, :], (H_A, W_A)), 0.0)
             for t in range(n_tok)]
    return jnp.concatenate(parts, axis=0)


def _moba_sample_k_kernel(pt_ref, q_ref, kn_ref, rbr_ref, bkl_ref, *rest, u, nchunk, ds):
    kpages = rest[:u]
    p_ref, st_ref = rest[u], rest[u + 1]
    qbd_ref, s_ref, ks_ref = rest[u + 2:]
    c = pl.program_id(1)
    nrow = ds * H_A
    bpc = u // 2
    nbp = nchunk * bpc

    @pl.when(c == 0)
    def _():
        qbd_ref[...] = _head_rows(q_ref[...], ds).astype(BF16)
        ks_ref[...] = jnp.zeros_like(ks_ref)

    qbd = qbd_ref[...]
    for j in range(u):
        s_ref[c, :, j * PAGE_SIZE:(j + 1) * PAGE_SIZE] = _dot(qbd, kpages[j][...].astype(BF16))
    ks = ks_ref[...]
    lane_blk = lax.broadcasted_iota(jnp.int32, ks.shape, 1)
    for v in range(bpc):
        col = jnp.sum(kpages[2 * v][...] + kpages[2 * v + 1][...], axis=1, keepdims=True)
        ks = jnp.where(lane_blk == c * bpc + v, col, ks)
    ks_ref[...] = ks

    @pl.when(c == nchunk - 1)
    def _():
        blk = MOBA_BLOCK
        km = (ks_ref[...] * (1.0 / blk)).astype(BF16)
        g = _dot(qbd, km)
        lane = lax.broadcasted_iota(jnp.int32, g.shape, 1)
        g = jnp.where(lane < nbp, g, -jnp.inf)
        selected = jnp.zeros(g.shape, jnp.bool_)
        for _ in range(min(MOBA_TOPK, nbp)):
            mx = jnp.max(g, axis=1, keepdims=True)
            first = jnp.min(jnp.where(g == mx, lane, nbp), axis=1, keepdims=True)
            hit = (lane == first) & (mx > -jnp.inf)
            selected = selected | hit
            g = jnp.where(hit, -jnp.inf, g)
        maskadd = jnp.where(selected, 0.0, NEG)

        rbr = rbr_ref[...]
        far = rbr[:, REL_BUCKETS - 1:REL_BUCKETS]
        bkl = bkl_ref[...]
        bias_last = jnp.zeros((nrow, blk), F32)
        for b in range(REL_BUCKETS):
            bias_last = jnp.where(bkl == b, rbr[:, b:b + 1], bias_last)

        trow = lax.broadcasted_iota(jnp.int32, (nrow, 1), 0) // H_A
        qf = qbd.astype(F32)
        knb = kn_ref[...].astype(BF16).astype(F32)
        own = []
        for cc in range(ds):
            sc = jnp.sum(qf * knb[cc:cc + 1, :], axis=1, keepdims=True)
            bias = jnp.zeros((nrow, 1), F32)
            for d in range(ds - cc):
                bias = jnp.where(trow - cc == d, rbr[:, d:d + 1], bias)
            own.append(jnp.where(trow >= cc, sc + bias, NEG))

        def logits(n):
            cn, off = n // bpc, (n % bpc) * blk
            b = bias_last if n == nbp - 1 else far
            return s_ref[cn, :, off:off + blk] + b + maskadd[:, n:n + 1]

        mt = logits(0)
        for n in range(1, nbp):
            mt = jnp.maximum(mt, logits(n))
        m = jnp.max(mt, axis=1, keepdims=True)
        for cc in range(ds):
            m = jnp.maximum(m, own[cc])
        lt = jnp.zeros((nrow, blk), F32)
        for n in range(nbp):
            cn, off = n // bpc, (n % bpc) * blk
            pn = jnp.exp(logits(n) - m)
            lt = lt + pn
            p_ref[cn, :, off:off + blk] = pn.astype(BF16)
        l = jnp.sum(lt, axis=1, keepdims=True)
        lane128 = lax.broadcasted_iota(jnp.int32, (nrow, 128), 1)
        stats = jnp.zeros((nrow, 128), F32)
        for cc in range(ds):
            po = jnp.exp(own[cc] - m)
            l = l + po
            stats = jnp.where(lane128 == cc, po, stats)
        st_ref[...] = jnp.where(lane128 == ds, l, stats)


def _moba_sample_v_kernel(pt_ref, p_ref, st_ref, vn_ref, *rest, u, nchunk, ds):
    vpages = rest[:u]
    o_ref, acc_ref = rest[u], rest[u + 1]
    c = pl.program_id(1)

    @pl.when(c == 0)
    def _():
        acc_ref[...] = jnp.zeros_like(acc_ref)

    acc = acc_ref[...]
    for j in range(u):
        acc = acc + _dot_nt(p_ref[:, j * PAGE_SIZE:(j + 1) * PAGE_SIZE], vpages[j][...].astype(BF16))
    acc_ref[...] = acc

    @pl.when(c == nchunk - 1)
    def _():
        a = acc_ref[...]
        st = st_ref[...]
        vnb = vn_ref[...].astype(BF16).astype(F32)
        for cc in range(ds):
            a = a + st[:, cc:cc + 1].astype(BF16).astype(F32) * vnb[cc:cc + 1, :]
        a = a / st[:, ds:ds + 1]
        lane_head = lax.broadcasted_iota(jnp.int32, (H_A, W_A), 1) // HD_A
        sub = lax.broadcasted_iota(jnp.int32, (H_A, W_A), 0)
        for t in range(ds):
            rows = a[t * H_A:(t + 1) * H_A, :]
            o_ref[t:t + 1, :] = jnp.sum(jnp.where(lane_head == sub, rows, 0.0), axis=0, keepdims=True)


def _moba_sample(layer, rel_bias, page_table, cache_k, cache_v, q, k_new, v_new):
    b, ds, _ = q.shape
    n_pages = page_table.shape[1]
    past_len = n_pages * PAGE_SIZE
    assert past_len % MOBA_BLOCK == 0 and ds <= PAGE_SIZE - 1
    nbp = past_len // MOBA_BLOCK
    u = min(32, n_pages)
    assert n_pages % u == 0 and u % 2 == 0 and nbp <= 128
    nchunk = n_pages // u
    nrow = ds * H_A
    n_pool, depth = cache_k.shape[:2]
    ck = cache_k.transpose(0, 1, 3, 4, 2).reshape(n_pool, depth, W_A, PAGE_SIZE)
    cv = cache_v.transpose(0, 1, 3, 4, 2).reshape(n_pool, depth, W_A, PAGE_SIZE)
    rbr = jnp.tile(rel_bias.T, (ds, 1))
    trow = np.arange(nrow)[:, None] // H_A
    bk_last = _rel_bucket_np(MOBA_BLOCK + trow - np.arange(MOBA_BLOCK)[None, :])

    def page_spec(j):
        return pl.BlockSpec((None, None, W_A, PAGE_SIZE),
                            lambda bi, c, pt: (pt[bi, c * u + j], layer, 0, 0))

    per_seq = lambda r, n: pl.BlockSpec((None, r, n), lambda bi, c, pt: (bi, 0, 0))
    const = lambda r, n: pl.BlockSpec((r, n), lambda bi, c, pt: (0, 0))
    probs, stats = pl.pallas_call(
        functools.partial(_moba_sample_k_kernel, u=u, nchunk=nchunk, ds=ds),
        grid_spec=pltpu.PrefetchScalarGridSpec(
            num_scalar_prefetch=1, grid=(b, nchunk),
            in_specs=[per_seq(ds, W_A), per_seq(ds, W_A), const(nrow, REL_BUCKETS), const(nrow, MOBA_BLOCK)]
                     + [page_spec(j) for j in range(u)],
            out_specs=[pl.BlockSpec((None, nchunk, nrow, u * PAGE_SIZE), lambda bi, c, pt: (bi, 0, 0, 0)),
                       per_seq(nrow, 128)],
            scratch_shapes=[pltpu.VMEM((nrow, W_A), BF16), pltpu.VMEM((nchunk, nrow, u * PAGE_SIZE), F32),
                            pltpu.VMEM((W_A, 128), F32)]),
        out_shape=[jax.ShapeDtypeStruct((b, nchunk, nrow, u * PAGE_SIZE), BF16),
                   jax.ShapeDtypeStruct((b, nrow, 128), F32)],
        compiler_params=_cparams(("arbitrary", "arbitrary")),
        name="moba_sample_keys",
    )(page_table, q, k_new, rbr, jnp.asarray(bk_last), *([ck] * u))
    return pl.pallas_call(
        functools.partial(_moba_sample_v_kernel, u=u, nchunk=nchunk, ds=ds),
        grid_spec=pltpu.PrefetchScalarGridSpec(
            num_scalar_prefetch=1, grid=(b, nchunk),
            in_specs=[pl.BlockSpec((None, None, nrow, u * PAGE_SIZE), lambda bi, c, pt: (bi, c, 0, 0)),
                      per_seq(nrow, 128), per_seq(ds, W_A)] + [page_spec(j) for j in range(u)],
            out_specs=per_seq(ds, W_A),
            scratch_shapes=[pltpu.VMEM((nrow, W_A), F32)]),
        out_shape=jax.ShapeDtypeStruct((b, ds, W_A), F32),
        compiler_params=_cparams(("arbitrary", "arbitrary")),
        name="moba_sample_values",
    )(page_table, probs, stats, v_new, *([cv] * u))


def _mix_kernel(x_ref, ya_ref, glu_ref, uc_ref, go_ref, hb_ref, hc_ref, cbw_ref, cbb_ref, lbg_ref, lbb_ref,
                ccw_ref, wo_ref, g1_ref, b1_ref, o_ref, extb, extc, shb, *, tm, shift, hbb, hbc, alpha):
    @pl.when(pl.program_id(0) == 0)
    def _():
        extb[0:hbb, :] = hb_ref[...]
        extc[0:hbc, :] = hc_ref[...]

    mix_a = _dot(ya_ref[...].astype(BF16), wo_ref[0:W_A, :])
    extb[hbb:hbb + tm, :] = glu_ref[...]
    extc[hbc:hbc + tm, :] = uc_ref[...]
    offb = hbb - (CONV_B - 1) * shift
    if shift != 1:
        tap = lambda w: extb[offb + w * shift:offb + w * shift + tm, :]
    else:
        for r in range(SUBLANES):
            rows = tm + SUBLANES * ((CONV_B - 1 - r) // SUBLANES)
            shb[r, 0:rows, :] = extb[offb + r:offb + r + rows, :]
        tap = lambda w: shb[w % SUBLANES, w - w % SUBLANES:w - w % SUBLANES + tm, :]
    cb = jnp.broadcast_to(cbb_ref[...], (tm, W_B))
    for w in range(CONV_B):
        cb = cb + tap(w) * cbw_ref[w:w + 1, :]
    yb = _layer_norm(cb, lbg_ref[...], lbb_ref[...])
    yb = yb * _sigmoid(yb)
    offc = hbc - (CONV_C - 1) * shift
    cc = extc[offc:offc + tm, :] * ccw_ref[0:1, :]
    for w in range(1, CONV_C):
        cc = cc + extc[offc + w * shift:offc + w * shift + tm, :] * ccw_ref[w:w + 1, :]
    yc = go_ref[...] * cc
    mix = (mix_a
           + _dot(yb.astype(BF16), wo_ref[W_A:W_A + W_B, :])
           + _dot(yc.astype(BF16), wo_ref[W_A + W_B:W_A + W_B + W_C, :]))
    o_ref[...] = _layer_norm(alpha * x_ref[...] + mix, g1_ref[...], b1_ref[...])
    nb_rows = extb[tm:tm + hbb, :]
    nc_rows = extc[tm:tm + hbc, :]
    extb[0:hbb, :] = nb_rows
    extc[0:hbc, :] = nc_rows


def _mix(x, ya, glu, uc, go, hist_b, hist_c, lw, *, shift, alpha):
    t = x.shape[0]
    tm = min(512, t)
    assert t % tm == 0
    hbb, hbc = hist_b.shape[0], hist_c.shape[0]
    row = lambda n: pl.BlockSpec((tm, n), lambda i: (i, 0))
    return pl.pallas_call(
        functools.partial(_mix_kernel, tm=tm, shift=shift, hbb=hbb, hbc=hbc, alpha=alpha),
        grid=(t // tm,),
        in_specs=[row(D_MODEL), row(W_A), row(W_B), row(W_C), row(W_C), _full((hbb, W_B)), _full((hbc, W_C)),
                  _full((CONV_B, W_B)), _full((1, W_B)), _full((1, W_B)), _full((1, W_B)), _full((CONV_C, W_C)),
                  _full((D_MODEL, D_MODEL)), _full((1, D_MODEL)), _full((1, D_MODEL))],
        out_specs=row(D_MODEL),
        out_shape=jax.ShapeDtypeStruct((t, D_MODEL), F32),
        scratch_shapes=[pltpu.VMEM((hbb + tm, W_B), F32), pltpu.VMEM((hbc + tm, W_C), F32),
                        pltpu.VMEM((SUBLANES, tm + SUBLANES * ((CONV_B - 1) // SUBLANES), W_B), F32)],
        compiler_params=_cparams(("arbitrary",)),
        name="mix_ln1",
    )(x, ya, glu, uc, go, hist_b, hist_c, lw['convb_w'], lw['convb_b'], lw['lnb_g'], lw['lnb_b'],
      lw['convc_w'], lw['w_out'], lw['ln1_g'], lw['ln1_b'])


def _cross_kernel(x_ref, mk_ref, mv_ref, wq_ref, wo_ref, g_ref, b_ref, o_ref, qx_ref, acc_ref,
                  *, tm, n_groups, alpha):
    g = pl.program_id(1)

    @pl.when(g == 0)
    def _():
        qx_ref[...] = (_dot(x_ref[...].astype(BF16), wq_ref[...]) * (HD_X ** -0.5)).astype(BF16)
        acc_ref[...] = jnp.zeros_like(acc_ref)

    mk = mk_ref[...].astype(BF16)
    mv = mv_ref[...].astype(BF16)
    heads = []
    head_cols = [slice(h * HD_X, (h + 1) * HD_X) for h in range(H_X)]
    scores = [_dot_nt(qx_ref[:, cols], mk[:, cols]) for cols in head_cols]
    for cols, s in zip(head_cols, scores):
        s = s - jnp.max(s, axis=-1, keepdims=True)
        e = jnp.exp(s)
        pr = e / jnp.sum(e, axis=-1, keepdims=True)
        heads.append(_dot(pr.astype(BF16), mv[:, cols]))
    o = jnp.concatenate(heads, axis=-1)
    if n_groups == 1:
        acc_ref[...] = o
    else:
        rgrp = (pl.program_id(0) * tm + lax.broadcasted_iota(jnp.int32, (tm, 1), 0)) % n_groups
        acc_ref[...] = jnp.where(rgrp == g, o, acc_ref[...])

    @pl.when(g == n_groups - 1)
    def _():
        ox = _dot(acc_ref[...].astype(BF16), wo_ref[...])
        o_ref[...] = _layer_norm(alpha * x_ref[...] + ox, g_ref[...], b_ref[...])


def _cross(x, mem_k, mem_v, lw, *, alpha):
    t = x.shape[0]
    n_groups, n_mem, _ = mem_k.shape
    tm = min(512, t)
    assert t % tm == 0
    row = pl.BlockSpec((tm, D_MODEL), lambda i, g: (i, 0))
    mem = pl.BlockSpec((None, n_mem, D_MODEL), lambda i, g: (g, 0, 0))
    cst = lambda r, n: pl.BlockSpec((r, n), lambda i, g: (0, 0))
    return pl.pallas_call(
        functools.partial(_cross_kernel, tm=tm, n_groups=n_groups, alpha=alpha),
        grid=(t // tm, n_groups),
        in_specs=[row, mem, mem, cst(D_MODEL, D_MODEL), cst(D_MODEL, D_MODEL), cst(1, D_MODEL), cst(1, D_MODEL)],
        out_specs=row,
        out_shape=jax.ShapeDtypeStruct((t, D_MODEL), F32),
        scratch_shapes=[pltpu.VMEM((tm, D_MODEL), BF16), pltpu.VMEM((tm, D_MODEL), F32)],
        compiler_params=_cparams(("arbitrary", "arbitrary")),
        name="cross_ln2",
    )(x, mem_k, mem_v, lw['w_qx'], lw['w_ox'], lw['ln2_g'], lw['ln2_b'])


def _ffn_kernel(x_ref, hf_ref, wg_ref, wu_ref, cfw_ref, cfb_ref, wd_ref, g_ref, b_ref, o_ref, st_ref,
                carry, ext, acc_ref, *, tm, shift, hbf, alpha):
    @pl.when(pl.program_id(0) == 0)
    def _():
        carry[...] = hf_ref[...]

    xb = x_ref[...].astype(BF16)
    off = hbf - (CONV_F - 1) * shift
    n_chunks = D_FF // FF_CHUNK

    def gate_up(c):
        cols = slice(c * FF_CHUNK, (c + 1) * FF_CHUNK)
        return _dot(xb, wg_ref[:, cols]), _dot(xb, wu_ref[:, cols])

    nxt = gate_up(0)
    for c in range(n_chunks):
        cols = slice(c * FF_CHUNK, (c + 1) * FF_CHUNK)
        gate, up = nxt
        if c + 1 < n_chunks:
            nxt = gate_up(c + 1)
        ext[0:hbf, :] = carry[:, cols]
        ext[hbf:hbf + tm, :] = gate
        gf = jnp.broadcast_to(cfb_ref[:, cols], (tm, FF_CHUNK))
        for w in range(CONV_F):
            gf = gf + ext[off + w * shift:off + w * shift + tm, :] * cfw_ref[w:w + 1, cols]
        carry[:, cols] = ext[tm:tm + hbf, :]
        hid = gf * _sigmoid(gf) * up
        part = _dot(hid.astype(BF16), wd_ref[cols, :])
        if c == 0:
            acc_ref[...] = part
        else:
            acc_ref[...] += part
    o_ref[...] = _layer_norm(alpha * x_ref[...] + acc_ref[...], g_ref[...], b_ref[...])
    st_ref[...] = carry[...]


def _ffn(x, hist_f, lw, *, shift, alpha):
    t = x.shape[0]
    tm = min(512, t)
    assert t % tm == 0
    hbf = hist_f.shape[0]
    row = pl.BlockSpec((tm, D_MODEL), lambda i: (i, 0))
    return pl.pallas_call(
        functools.partial(_ffn_kernel, tm=tm, shift=shift, hbf=hbf, alpha=alpha),
        grid=(t // tm,),
        in_specs=[row, _full((hbf, D_FF)), _full((D_MODEL, D_FF)), _full((D_MODEL, D_FF)),
                  _full((CONV_F, D_FF)), _full((1, D_FF)), _full((D_FF, D_MODEL)), _full((1, D_MODEL)),
                  _full((1, D_MODEL))],
        out_specs=[row, _full((hbf, D_FF))],
        out_shape=[jax.ShapeDtypeStruct((t, D_MODEL), F32), jax.ShapeDtypeStruct((hbf, D_FF), F32)],
        scratch_shapes=[pltpu.VMEM((hbf, D_FF), F32), pltpu.VMEM((hbf + tm, FF_CHUNK), F32),
                        pltpu.VMEM((tm, D_MODEL), F32)],
        compiler_params=_cparams(("arbitrary",)),
        name="ffn_ln3",
    )(x, hist_f, lw['w_gate'], lw['w_up'], lw['convf_w'], lw['convf_b'], lw['w_down'], lw['ln3_g'], lw['ln3_b'])


def _mem_proj_kernel(m_ref, wk_ref, wv_ref, k_ref, v_ref):
    mb = m_ref[...].astype(BF16)
    k_ref[...] = _dot(mb, wk_ref[...])
    v_ref[...] = _dot(mb, wv_ref[...])


def _mem_proj(mem, wk_bf, wv_bf):
    n = mem.shape[0]
    sd = jax.ShapeDtypeStruct((n, D_MODEL), F32)
    return pl.pallas_call(
        _mem_proj_kernel, out_shape=[sd, sd],
        compiler_params=pltpu.CompilerParams(vmem_limit_bytes=V7X_VMEM_LIMIT),
        name="mem_proj",
    )(mem, wk_bf, wv_bf)


def _pad_front(rows, n, width):
    return jnp.concatenate([jnp.zeros((n - rows.shape[0], width), F32), rows], axis=0)


def kernel(x_prompt, x_sample, mem_prompt, cache_k, cache_v, page_table, cache_mem_k, cache_mem_v, state_conv_b, state_conv_c, state_conv_f, rel_bias, w_in, w_out, convb_w, convb_b, lnb_g, lnb_b, convc_w, ln1_g, ln1_b, w_qx, w_kx, w_vx, w_ox, ln2_g, ln2_b, w_gate, w_up, convf_w, convf_b, w_down, ln3_g, ln3_b):
    bp, seq, _ = x_prompt.shape
    bs, ds, _ = x_sample.shape
    depth = w_in.shape[0]
    n_mem = mem_prompt.shape[1]
    alpha = _alpha(depth)
    assert bp == 1 and seq % MOBA_BLOCK == 0

    xp = x_prompt.reshape(seq, D_MODEL)
    xs = x_sample.transpose(1, 0, 2).reshape(ds * bs, D_MODEL)
    mem_p = mem_prompt.reshape(n_mem, D_MODEL)
    zeros_b = jnp.zeros((32, W_B), F32)
    zeros_c = jnp.zeros((8, W_C), F32)
    zeros_f = jnp.zeros((8, D_FF), F32)

    outs = {n: [] for n in ('kp', 'vp', 'ks', 'vs', 'mkp', 'mvp', 'cbp', 'cbs', 'ccp', 'ccs', 'cfp', 'cfs')}
    for l in range(depth):
        vec = lambda a: a[l].reshape(1, -1)
        lw = {'w_out': w_out[l].astype(BF16), 'convb_w': convb_w[l], 'convb_b': vec(convb_b),
              'lnb_g': vec(lnb_g), 'lnb_b': vec(lnb_b), 'convc_w': convc_w[l], 'ln1_g': vec(ln1_g),
              'ln1_b': vec(ln1_b), 'w_qx': w_qx[l].astype(BF16), 'w_ox': w_ox[l].astype(BF16),
              'ln2_g': vec(ln2_g), 'ln2_b': vec(ln2_b), 'w_gate': w_gate[l].astype(BF16),
              'w_up': w_up[l].astype(BF16), 'convf_w': convf_w[l], 'convf_b': vec(convf_b),
              'w_down': w_down[l].astype(BF16), 'ln3_g': vec(ln3_g), 'ln3_b': vec(ln3_b)}
        w_in_bf = w_in[l].astype(BF16)

        mk, mv = _mem_proj(mem_p, w_kx[l].astype(BF16), w_vx[l].astype(BF16))
        qt, kb, vt, k, v, glu, uc, go, kmean = _in_proj(xp, w_in_bf, prompt=True)
        ya = _moba_prompt(rel_bias, qt, kb, vt, kmean)
        x1 = _mix(xp, ya, glu, uc, go, zeros_b, zeros_c, lw, shift=1, alpha=alpha)
        x2 = _cross(x1, mk[None], mv[None], lw, alpha=alpha)
        xp, st_f = _ffn(x2, zeros_f, lw, shift=1, alpha=alpha)
        from_rows = lambda a: a.reshape(1, H_A, HD_A, seq).transpose(0, 3, 1, 2)
        outs['kp'].append(from_rows(k))
        outs['vp'].append(from_rows(v))
        outs['mkp'].append(mk.reshape(1, n_mem, H_X, HD_X))
        outs['mvp'].append(mv.reshape(1, n_mem, H_X, HD_X))
        outs['cbp'].append(glu[None, seq - (CONV_B - 1):])
        outs['ccp'].append(uc[None, seq - (CONV_C - 1):])
        outs['cfp'].append(st_f[None, st_f.shape[0] - (CONV_F - 1):])

        hb = state_conv_b[l].transpose(1, 0, 2)
        hc = state_conv_c[l].transpose(1, 0, 2)
        hf = state_conv_f[l].transpose(1, 0, 2)
        q, k, v, glu, uc, go = _in_proj(xs, w_in_bf, prompt=False)
        seq_major = lambda a: a.reshape(ds, bs, -1).transpose(1, 0, 2)
        k_sm, v_sm = seq_major(k), seq_major(v)
        ya = _moba_sample(l, rel_bias, page_table, cache_k, cache_v, seq_major(q), k_sm, v_sm)
        ya = ya.transpose(1, 0, 2).reshape(ds * bs, W_A)
        x1 = _mix(xs, ya, glu, uc, go, hb.reshape(-1, W_B), hc.reshape(-1, W_C), lw, shift=bs, alpha=alpha)
        x2 = _cross(x1, cache_mem_k[l].reshape(bs, n_mem, D_MODEL), cache_mem_v[l].reshape(bs, n_mem, D_MODEL),
                    lw, alpha=alpha)
        xs, st_f = _ffn(x2, hf.reshape(-1, D_FF), lw, shift=bs, alpha=alpha)
        outs['ks'].append(k_sm.reshape(bs, ds, H_A, HD_A))
        outs['vs'].append(v_sm.reshape(bs, ds, H_A, HD_A))
        tail = lambda hist, new, n: jnp.concatenate([hist, new.reshape(ds, bs, -1)], axis=0)[-n:].transpose(1, 0, 2)
        outs['cbs'].append(tail(hb, glu, CONV_B - 1))
        outs['ccs'].append(tail(hc, uc, CONV_C - 1))
        outs['cfs'].append(st_f.reshape(CONV_F - 1, bs, D_FF).transpose(1, 0, 2))

    y_prompt = xp.reshape(1, seq, D_MODEL)
    y_sample = xs.reshape(ds, bs, D_MODEL).transpose(1, 0, 2)
    st = lambda n, axis: jnp.stack(outs[n], axis=axis)
    return (y_prompt, y_sample, st('kp', 1), st('vp', 1), st('ks', 1), st('vs', 1), st('mkp', 0), st('mvp', 0),
            st('cbp', 0), st('cbs', 0), st('ccp', 0), st('ccs', 0), st('cfp', 0), st('cfs', 0))
```

```python
import functools
import math

import numpy as np
import jax
import jax.numpy as jnp
from jax import lax
from jax.experimental import pallas as pl
from jax.experimental.pallas import tpu as pltpu

F32 = jnp.float32
BF16 = jnp.bfloat16

D_MODEL = 1024
PAGE_SIZE = 128
HD_A = 64
W_A = 512
H_A = 8
W_B = 256
W_C = 256
PROJ_IN = 2816
MOBA_BLOCK = 256
MOBA_TOPK = 3
CONV_B = 31
CONV_C = 3
CONV_F = 3
D_FF = 2816
FF_CHUNK = 256
H_X = 4
HD_X = 256
REL_BUCKETS = 32
REL_MAX_DIST = 128
LN_EPS = 1e-5
SUBLANES = 8
LANES = 128
MEM_ROWS = H_X * (HD_X // LANES)

NEG = -1e30
VROWS = 80
FAR_UNROLL = 8
LOG2E = math.log2(math.e)
V7X_VMEM_LIMIT = 56 * 1024 * 1024


def _alpha(depth):
    return (2 * depth) ** 0.25


def _cparams(sem):
    return pltpu.CompilerParams(dimension_semantics=sem, vmem_limit_bytes=V7X_VMEM_LIMIT)


def _full(shape):
    return pl.BlockSpec(shape, lambda *_: (0,) * len(shape))


def _sigmoid(x):
    return 1.0 / (1.0 + jnp.exp(-x))


def _layer_norm(h, g, b):
    mu = jnp.mean(h, axis=-1, keepdims=True)
    d = h - mu
    var = jnp.mean(d * d, axis=-1, keepdims=True)
    return d * lax.rsqrt(var + LN_EPS) * g + b


def _dot(a, b):
    return jnp.dot(a, b, preferred_element_type=F32)


def _dot_nt(a, b):
    return lax.dot_general(a, b, (((1,), (1,)), ((), ())), preferred_element_type=F32)


def _rel_bucket_np(dist):
    n = np.maximum(dist, 0)
    max_exact = REL_BUCKETS // 2
    nf = np.maximum(n, max_exact).astype(np.float32)
    large = max_exact + (np.log(nf / np.float32(max_exact)) / np.float32(math.log(REL_MAX_DIST / max_exact))
                         * np.float32(REL_BUCKETS - max_exact)).astype(np.int32)
    return np.where(n < max_exact, n, np.minimum(large, REL_BUCKETS - 1)).astype(np.int32)


def _in_proj_kernel(x_ref, w_ref, *outs, prompt, tm):
    xb = x_ref[...].astype(BF16)

    def proj(lo, hi):
        return _dot(xb, w_ref[:, lo:hi])

    q = proj(0, W_A) * (HD_A ** -0.5 * (LOG2E if prompt else 1.0))
    k = proj(W_A, 2 * W_A)
    v = proj(2 * W_A, 3 * W_A)
    o = 3 * W_A
    glu = proj(o, o + W_B) * _sigmoid(proj(o + W_B, o + 2 * W_B))
    o += 2 * W_B
    gate_out = proj(o, o + W_C)
    uc = proj(o + W_C, o + 2 * W_C) * proj(o + 2 * W_C, o + 3 * W_C)
    if prompt:
        qt_ref, kb_ref, vt_ref, k_ref, v_ref, glu_ref, uc_ref, go_ref, km_ref = outs
        for c in range(W_A // 128):
            cols = slice(c * 128, (c + 1) * 128)
            qt_ref[cols, :] = q[:, cols].T.astype(BF16)
            k_ref[cols, :] = k[:, cols].T
            v_ref[cols, :] = v[:, cols].T
        kb_ref[...] = k.astype(BF16)
        for s in range(tm // MOBA_BLOCK):
            rows = slice(s * MOBA_BLOCK, (s + 1) * MOBA_BLOCK)
            for c in range(W_A // 128):
                vt2 = v[rows, c * 128:(c + 1) * 128].T.astype(BF16)
                for hh in range(2):
                    r0 = (2 * c + hh) * VROWS
                    vt_ref[s, r0:r0 + HD_A, :] = vt2[hh * HD_A:(hh + 1) * HD_A, :]
                    vt_ref[s, r0 + HD_A:r0 + VROWS, :] = jnp.ones((VROWS - HD_A, MOBA_BLOCK), BF16)
            km_ref[s] = jnp.sum(k[rows, :], axis=0, keepdims=True) * (1.0 / MOBA_BLOCK)
    else:
        q_ref, k_ref, v_ref, glu_ref, uc_ref, go_ref = outs
        q_ref[...] = q
        k_ref[...] = k
        v_ref[...] = v
    glu_ref[...] = glu
    uc_ref[...] = uc
    go_ref[...] = gate_out


def _in_proj(x, w_bf, *, prompt):
    t = x.shape[0]
    tm = min(512, t)
    assert t % tm == 0
    nb = t // MOBA_BLOCK
    row = lambda n: pl.BlockSpec((tm, n), lambda i: (i, 0))
    sd = jax.ShapeDtypeStruct
    if prompt:
        assert tm % MOBA_BLOCK == 0
        spb = tm // MOBA_BLOCK
        out_shape = [sd((W_A, t), BF16), sd((t, W_A), BF16), sd((nb, H_A * VROWS, MOBA_BLOCK), BF16),
                     sd((W_A, t), F32), sd((W_A, t), F32), sd((t, W_B), F32), sd((t, W_C), F32),
                     sd((t, W_C), F32), sd((nb, 1, W_A), F32)]
        col = pl.BlockSpec((W_A, tm), lambda i: (0, i))
        out_specs = [col, row(W_A),
                     pl.BlockSpec((spb, H_A * VROWS, MOBA_BLOCK), lambda i: (i, 0, 0)),
                     col, col, row(W_B), row(W_C), row(W_C),
                     pl.BlockSpec((spb, 1, W_A), lambda i: (i, 0, 0))]
    else:
        out_shape = [sd((t, W_A), F32), sd((t, W_A), F32), sd((t, W_A), F32), sd((t, W_B), F32),
                     sd((t, W_C), F32), sd((t, W_C), F32)]
        out_specs = [row(W_A), row(W_A), row(W_A), row(W_B), row(W_C), row(W_C)]
    return pl.pallas_call(
        functools.partial(_in_proj_kernel, prompt=prompt, tm=tm),
        grid=(t // tm,),
        in_specs=[row(D_MODEL), _full((D_MODEL, PROJ_IN))],
        out_specs=out_specs, out_shape=out_shape,
        compiler_params=_cparams(("arbitrary",)),
        name="in_proj_prompt" if prompt else "in_proj_sample",
    )(x, w_bf)


def _moba_prompt_kernel(*refs, nb):
    _moba_prompt_step(pl.program_id(0), pl.program_id(1), *refs, nb=nb)


def _moba_prompt_step(p, i, rb_ref, bko_ref, bkp_ref, qt_ref, k_ref, vt_ref, km_ref, o_ref,
                      bown, bprev, sel_ref, m_ref, acc_ref, sping, spong, *, nb):
    blk = MOBA_BLOCK

    @pl.when(i == 0)
    def _():
        bko = bko_ref[...]
        bkp = bkp_ref[...]
        for hh in range(2):
            h = 2 * p + hh
            far = rb_ref[REL_BUCKETS - 1, h]
            bo = jnp.full((blk, blk), NEG, F32)
            bp = jnp.zeros((blk, blk), F32)
            for b in range(REL_BUCKETS):
                val = (rb_ref[b, h] - far) * LOG2E
                bo = jnp.where(bko == b, val, bo)
                bp = jnp.where(bkp == b, val, bp)
            bown[hh] = bo
            bprev[hh] = bp

    qt = qt_ref[...]
    rowi = lax.broadcasted_iota(jnp.int32, qt.shape, 0)
    zero = jnp.zeros_like(qt)
    qts = [jnp.where(rowi < HD_A, qt, zero), jnp.where(rowi >= HD_A, qt, zero)]

    km = km_ref[...].astype(BF16)
    bidx = lax.broadcasted_iota(jnp.int32, (nb, blk), 0)
    for hh in range(2):
        g = _dot(km, qts[hh])
        g = jnp.where(bidx < i, g, -jnp.inf)
        selected = jnp.zeros((nb, blk), jnp.bool_)
        for _ in range(MOBA_TOPK):
            mx = jnp.max(g, axis=0, keepdims=True)
            first = jnp.min(jnp.where(g == mx, bidx, nb), axis=0, keepdims=True)
            hit = (bidx == first) & (mx > -jnp.inf)
            selected = selected | hit
            g = jnp.where(hit, -jnp.inf, g)
        sel_ref[hh] = jnp.where(selected, 0.0, NEG)
        m_ref[hh] = jnp.full((1, blk), NEG, F32)
        acc_ref[hh] = jnp.zeros((VROWS, blk), F32)

    def scores(j, hh):
        return _dot(k_ref[j], qts[hh])

    def softmax_pv(s, j, hh, colbias):
        mx = jnp.max(s, axis=0, keepdims=True)
        if colbias is not None:
            mx = mx + colbias
        m_old = m_ref[hh]
        m_new = jnp.maximum(m_old, mx)
        a = jnp.exp2(m_old - m_new)
        c = m_new if colbias is None else m_new - colbias
        pt = jnp.exp2(s - c).astype(BF16)
        r = _dot(vt_ref[j, hh * VROWS:(hh + 1) * VROWS, :], pt)
        acc_ref[hh] = a * acc_ref[hh] + r
        m_ref[hh] = m_new

    n_far = jnp.maximum(i - 1, 0)
    jp = jnp.maximum(i - 1, 0)
    s_own = [scores(i, hh) + bown[hh] for hh in range(2)]
    s_prev = [scores(jp, hh) + bprev[hh] for hh in range(2)]
    for hh in range(2):
        sping[hh] = scores(0, hh)
    for hh in range(2):
        softmax_pv(s_own[hh], i, hh, None)
    for hh in range(2):
        prev_mask = jnp.where(i >= 1, sel_ref[hh, pl.ds(jp, 1), :], NEG)
        softmax_pv(s_prev[hh], jp, hh, prev_mask)

    bufs = [sping, spong]

    def far_blocks(t, carry):
        last = n_far - 1
        for u in range(FAR_UNROLL):
            cur, nxt = bufs[u % 2], bufs[1 - u % 2]
            j = t * FAR_UNROLL + u
            jn = jnp.minimum(j + 1, last)
            for hh in range(2):
                nxt[hh] = scores(jn, hh)
            jc = jnp.minimum(j, last)
            for hh in range(2):
                mask = jnp.where(j <= last, sel_ref[hh, pl.ds(jc, 1), :], NEG)
                softmax_pv(cur[hh], jc, hh, mask)
        return carry

    lax.fori_loop(0, (n_far + FAR_UNROLL - 1) // FAR_UNROLL, far_blocks, 0)

    outs = []
    for hh in range(2):
        acc = acc_ref[hh]
        outs.append(acc[0:HD_A, :] / acc[HD_A:HD_A + 1, :])
    o_ref[...] = jnp.concatenate(outs, axis=0).T


def _moba_prompt(rel_bias, qt, kb, vt, kmean):
    t = qt.shape[1]
    nb = t // MOBA_BLOCK
    blk = MOBA_BLOCK
    c = np.arange(blk)[:, None]
    r = np.arange(blk)[None, :]
    bk_own = np.where(c <= r, _rel_bucket_np(r - c), -1).astype(np.int32)
    bk_prev = _rel_bucket_np(blk + r - c)
    smem = pl.BlockSpec(memory_space=pltpu.SMEM)
    return pl.pallas_call(
        functools.partial(_moba_prompt_kernel, nb=nb),
        grid=(H_A // 2, nb),
        in_specs=[smem, _full((blk, blk)), _full((blk, blk)),
                  pl.BlockSpec((128, blk), lambda p, i: (p, i)),
                  pl.BlockSpec((nb, blk, 128), lambda p, i: (0, 0, p)),
                  pl.BlockSpec((nb, 2 * VROWS, blk), lambda p, i: (0, p, 0)),
                  pl.BlockSpec((nb, 128), lambda p, i: (0, p))],
        out_specs=pl.BlockSpec((blk, 128), lambda p, i: (i, p)),
        out_shape=jax.ShapeDtypeStruct((t, W_A), F32),
        scratch_shapes=[pltpu.VMEM((2, blk, blk), F32), pltpu.VMEM((2, blk, blk), F32),
                        pltpu.VMEM((2, nb, blk), F32), pltpu.VMEM((2, 1, blk), F32),
                        pltpu.VMEM((2, VROWS, blk), F32),
                        pltpu.VMEM((2, blk, blk), F32), pltpu.VMEM((2, blk, blk), F32)],
        compiler_params=_cparams(("arbitrary", "arbitrary")),
        name="moba_prompt",
    )(rel_bias, jnp.asarray(bk_own), jnp.asarray(bk_prev), qt, kb.reshape(nb, blk, W_A), vt,
      kmean.reshape(nb, W_A))


def _head_rows(x, n_tok):
    lane_head = lax.broadcasted_iota(jnp.int32, (H_A, W_A), 1) // HD_A
    sub = lax.broadcasted_iota(jnp.int32, (H_A, W_A), 0)
    parts = [jnp.where(lane_head == sub, jnp.broadcast_to(x[t:t + 1, :], (H_A, W_A)), 0.0)
             for t in range(n_tok)]
    return jnp.concatenate(parts, axis=0)


def _moba_sample_k_kernel(pt_ref, q_ref, kn_ref, rbr_ref, bkl_ref, *rest, u, nchunk, ds):
    kpages = rest[:u]
    p_ref, st_ref = rest[u], rest[u + 1]
    qbd_ref, s_ref, ks_ref = rest[u + 2:]
    _sample_keys_step(pl.program_id(1), q_ref, kn_ref, rbr_ref, bkl_ref, kpages, [p_ref], [st_ref],
                      qbd_ref, s_ref, ks_ref, u=u, nchunk=nchunk, ds=ds)


def _sample_keys_step(c, q_ref, kn_ref, rbr_ref, bkl_ref, kpages, p_refs, st_refs, qbd_ref, s_ref, ks_ref,
                      *, u, nchunk, ds):
    nrow = ds * H_A
    bpc = u // 2
    nbp = nchunk * bpc

    @pl.when(c == 0)
    def _():
        qbd_ref[...] = _head_rows(q_ref[...], ds).astype(BF16)
        ks_ref[...] = jnp.zeros_like(ks_ref)

    qbd = qbd_ref[...]
    for j in range(u):
        s_ref[c, :, j * PAGE_SIZE:(j + 1) * PAGE_SIZE] = _dot(qbd, kpages[j][...].astype(BF16))
    ks = ks_ref[...]
    lane_blk = lax.broadcasted_iota(jnp.int32, ks.shape, 1)
    for v in range(bpc):
        col = jnp.sum(kpages[2 * v][...] + kpages[2 * v + 1][...], axis=1, keepdims=True)
        ks = jnp.where(lane_blk == c * bpc + v, col, ks)
    ks_ref[...] = ks

    @pl.when(c == nchunk - 1)
    def _():
        blk = MOBA_BLOCK
        km = (ks_ref[...] * (1.0 / blk)).astype(BF16)
        g = _dot(qbd, km)
        lane = lax.broadcasted_iota(jnp.int32, g.shape, 1)
        g = jnp.where(lane < nbp, g, -jnp.inf)
        selected = jnp.zeros(g.shape, jnp.bool_)
        for _ in range(min(MOBA_TOPK, nbp)):
            mx = jnp.max(g, axis=1, keepdims=True)
            first = jnp.min(jnp.where(g == mx, lane, nbp), axis=1, keepdims=True)
            hit = (lane == first) & (mx > -jnp.inf)
            selected = selected | hit
            g = jnp.where(hit, -jnp.inf, g)
        maskadd = jnp.where(selected, 0.0, NEG)

        rbr = rbr_ref[...]
        far = rbr[:, REL_BUCKETS - 1:REL_BUCKETS]
        bkl = bkl_ref[...]
        bias_last = jnp.zeros((nrow, blk), F32)
        for b in range(REL_BUCKETS):
            bias_last = jnp.where(bkl == b, rbr[:, b:b + 1], bias_last)

        trow = lax.broadcasted_iota(jnp.int32, (nrow, 1), 0) // H_A
        qf = qbd.astype(F32)
        knb = kn_ref[...].astype(BF16).astype(F32)
        own = []
        for cc in range(ds):
            sc = jnp.sum(qf * knb[cc:cc + 1, :], axis=1, keepdims=True)
            bias = jnp.zeros((nrow, 1), F32)
            for d in range(ds - cc):
                bias = jnp.where(trow - cc == d, rbr[:, d:d + 1], bias)
            own.append(jnp.where(trow >= cc, sc + bias, NEG))

        def logits(n):
            cn, off = n // bpc, (n % bpc) * blk
            b = bias_last if n == nbp - 1 else far
            return s_ref[cn, :, off:off + blk] + b + maskadd[:, n:n + 1]

        mt = logits(0)
        for n in range(1, nbp):
            mt = jnp.maximum(mt, logits(n))
        m = jnp.max(mt, axis=1, keepdims=True)
        for cc in range(ds):
            m = jnp.maximum(m, own[cc])
        lt = jnp.zeros((nrow, blk), F32)
        for n in range(nbp):
            cn, off = n // bpc, (n % bpc) * blk
            pn = jnp.exp(logits(n) - m)
            lt = lt + pn
            for p_ref in p_refs:
                p_ref[cn, :, off:off + blk] = pn.astype(BF16)
        l = jnp.sum(lt, axis=1, keepdims=True)
        lane128 = lax.broadcasted_iota(jnp.int32, (nrow, 128), 1)
        stats = jnp.zeros((nrow, 128), F32)
        for cc in range(ds):
            po = jnp.exp(own[cc] - m)
            l = l + po
            stats = jnp.where(lane128 == cc, po, stats)
        for st_ref in st_refs:
            st_ref[...] = jnp.where(lane128 == ds, l, stats)


def _moba_sample_v_kernel(pt_ref, p_ref, st_ref, vn_ref, *rest, u, nchunk, ds):
    vpages = rest[:u]
    o_ref, acc_ref = rest[u], rest[u + 1]
    _sample_values_step(pl.program_id(1), p_ref, st_ref, vn_ref, vpages, o_ref, acc_ref, u=u, nchunk=nchunk, ds=ds)


def _sample_values_step(c, p_ref, st_ref, vn_ref, vpages, o_ref, acc_ref, *, u, nchunk, ds):

    @pl.when(c == 0)
    def _():
        acc_ref[...] = jnp.zeros_like(acc_ref)

    acc = acc_ref[...]
    for j in range(u):
        acc = acc + _dot_nt(p_ref[:, j * PAGE_SIZE:(j + 1) * PAGE_SIZE], vpages[j][...].astype(BF16))
    acc_ref[...] = acc

    @pl.when(c == nchunk - 1)
    def _():
        a = acc_ref[...]
        st = st_ref[...]
        vnb = vn_ref[...].astype(BF16).astype(F32)
        for cc in range(ds):
            a = a + st[:, cc:cc + 1].astype(BF16).astype(F32) * vnb[cc:cc + 1, :]
        a = a / st[:, ds:ds + 1]
        lane_head = lax.broadcasted_iota(jnp.int32, (H_A, W_A), 1) // HD_A
        sub = lax.broadcasted_iota(jnp.int32, (H_A, W_A), 0)
        for t in range(ds):
            rows = a[t * H_A:(t + 1) * H_A, :]
            o_ref[t:t + 1, :] = jnp.sum(jnp.where(lane_head == sub, rows, 0.0), axis=0, keepdims=True)


def _page_view(cache):
    n_pool, depth = cache.shape[:2]
    return cache.transpose(0, 1, 3, 4, 2).reshape(n_pool, depth, W_A, PAGE_SIZE)


def _sample_tables(rel_bias, ds):
    nrow = ds * H_A
    rbr = jnp.tile(rel_bias.T, (ds, 1))
    trow = np.arange(nrow)[:, None] // H_A
    bk_last = _rel_bucket_np(MOBA_BLOCK + trow - np.arange(MOBA_BLOCK)[None, :])
    return rbr, jnp.asarray(bk_last)


def _check_sample_shapes(n_pages, ds, u):
    past_len = n_pages * PAGE_SIZE
    assert past_len % MOBA_BLOCK == 0 and ds <= REL_BUCKETS // 2
    assert n_pages % u == 0 and u % 2 == 0 and past_len // MOBA_BLOCK <= LANES


def _moba_sample_values(layer, page_table, cv, probs, stats, v_new, u):
    b, ds, _ = v_new.shape
    nchunk = page_table.shape[1] // u
    nrow = ds * H_A
    page_spec = lambda j: pl.BlockSpec((None, None, W_A, PAGE_SIZE),
                                       lambda bi, c, pt: (pt[bi, c * u + j], layer, 0, 0))
    per_seq = lambda r, n: pl.BlockSpec((None, r, n), lambda bi, c, pt: (bi, 0, 0))
    return pl.pallas_call(
        functools.partial(_moba_sample_v_kernel, u=u, nchunk=nchunk, ds=ds),
        grid_spec=pltpu.PrefetchScalarGridSpec(
            num_scalar_prefetch=1, grid=(b, nchunk),
            in_specs=[pl.BlockSpec((None, None, nrow, u * PAGE_SIZE), lambda bi, c, pt: (bi, c, 0, 0)),
                      per_seq(nrow, 128), per_seq(ds, W_A)] + [page_spec(j) for j in range(u)],
            out_specs=per_seq(ds, W_A),
            scratch_shapes=[pltpu.VMEM((nrow, W_A), F32)]),
        out_shape=jax.ShapeDtypeStruct((b, ds, W_A), F32),
        compiler_params=_cparams(("arbitrary", "arbitrary")),
        name="moba_sample_values",
    )(page_table, probs, stats, v_new, *([cv] * u))


def _moba_sample(layer, rel_bias, page_table, cache_k, cache_v, q, k_new, v_new):
    b, ds, _ = q.shape
    n_pages = page_table.shape[1]
    u = min(32, n_pages)
    _check_sample_shapes(n_pages, ds, u)
    nchunk = n_pages // u
    nrow = ds * H_A
    ck, cv = _page_view(cache_k), _page_view(cache_v)
    rbr, bk_last = _sample_tables(rel_bias, ds)

    def page_spec(j):
        return pl.BlockSpec((None, None, W_A, PAGE_SIZE),
                            lambda bi, c, pt: (pt[bi, c * u + j], layer, 0, 0))

    per_seq = lambda r, n: pl.BlockSpec((None, r, n), lambda bi, c, pt: (bi, 0, 0))
    const = lambda r, n: pl.BlockSpec((r, n), lambda bi, c, pt: (0, 0))
    probs, stats = pl.pallas_call(
        functools.partial(_moba_sample_k_kernel, u=u, nchunk=nchunk, ds=ds),
        grid_spec=pltpu.PrefetchScalarGridSpec(
            num_scalar_prefetch=1, grid=(b, nchunk),
            in_specs=[per_seq(ds, W_A), per_seq(ds, W_A), const(nrow, REL_BUCKETS), const(nrow, MOBA_BLOCK)]
                     + [page_spec(j) for j in range(u)],
            out_specs=[pl.BlockSpec((None, nchunk, nrow, u * PAGE_SIZE), lambda bi, c, pt: (bi, 0, 0, 0)),
                       per_seq(nrow, 128)],
            scratch_shapes=[pltpu.VMEM((nrow, W_A), BF16), pltpu.VMEM((nchunk, nrow, u * PAGE_SIZE), F32),
                            pltpu.VMEM((W_A, 128), F32)]),
        out_shape=[jax.ShapeDtypeStruct((b, nchunk, nrow, u * PAGE_SIZE), BF16),
                   jax.ShapeDtypeStruct((b, nrow, 128), F32)],
        compiler_params=_cparams(("arbitrary", "arbitrary")),
        name="moba_sample_keys",
    )(page_table, q, k_new, rbr, bk_last, *([ck] * u))
    return _moba_sample_values(layer, page_table, cv, probs, stats, v_new, u)


def _moba_fused_kernel(pt_ref, rb_ref, bko_ref, bkp_ref, qt_ref, k_ref, vt_ref, km_ref,
                       q_ref, kn_ref, vn_ref, rbr_ref, bkl_ref, *rest, nb, u, nchunk, ds):
    kpages, vpages = rest[:u], rest[u:2 * u]
    o_ref, p_out, st_out, y_ref = rest[2 * u:2 * u + 4]
    (bown, bprev, sel_ref, m_ref, acc_ref, sping, spong,
     qbd_ref, s_ref, ks_ref, p_scr, st_scr, accv_ref) = rest[2 * u + 4:]
    p = pl.program_id(0)
    i = pl.program_id(1)
    step = p * nb + i
    b = step // nchunk
    c = step % nchunk
    par = b % 2
    _sample_keys_step(c, q_ref, kn_ref, rbr_ref, bkl_ref, kpages, [p_out, p_scr.at[par]],
                      [st_out, st_scr.at[par]], qbd_ref, s_ref, ks_ref, u=u, nchunk=nchunk, ds=ds)

    @pl.when(b >= 1)
    def _():
        _sample_values_step(c, p_scr.at[1 - par, c], st_scr.at[1 - par], vn_ref, vpages, y_ref, accv_ref,
                            u=u, nchunk=nchunk, ds=ds)

    _moba_prompt_step(p, i, rb_ref, bko_ref, bkp_ref, qt_ref, k_ref, vt_ref, km_ref, o_ref,
                      bown, bprev, sel_ref, m_ref, acc_ref, sping, spong, nb=nb)


def _fused_pages_per_step(n_steps, bs, n_pages):
    if bs < 2 or (bs * n_pages) % n_steps:
        return None
    u = bs * n_pages // n_steps
    if u < 2 or u % 2 or u > 32 or n_pages % u:
        return None
    return u


def _moba_fused(layer, rel_bias, qt, kb, vt, kmean, page_table, cache_k, cache_v, q, k_new, v_new, u):
    t = qt.shape[1]
    nb = t // MOBA_BLOCK
    blk = MOBA_BLOCK
    bs, ds, _ = q.shape
    n_pages = page_table.shape[1]
    _check_sample_shapes(n_pages, ds, u)
    nchunk = n_pages // u
    assert bs * nchunk == (H_A // 2) * nb
    nrow = ds * H_A
    ck, cv = _page_view(cache_k), _page_view(cache_v)
    rbr, bk_last = _sample_tables(rel_bias, ds)
    c_ = np.arange(blk)[:, None]
    r_ = np.arange(blk)[None, :]
    bk_own = np.where(c_ <= r_, _rel_bucket_np(r_ - c_), -1).astype(np.int32)
    bk_prev = _rel_bucket_np(blk + r_ - c_)

    seq_k = lambda p, i: (p * nb + i) // nchunk
    seq_v = lambda p, i: jnp.maximum(seq_k(p, i) - 1, 0)
    chunk = lambda p, i: (p * nb + i) % nchunk
    const = lambda shape: pl.BlockSpec(shape, lambda p, i, pt: (0,) * len(shape))
    per_k = lambda r, n: pl.BlockSpec((None, r, n), lambda p, i, pt: (seq_k(p, i), 0, 0))
    per_v = lambda r, n: pl.BlockSpec((None, r, n), lambda p, i, pt: (seq_v(p, i), 0, 0))
    kpage = lambda j: pl.BlockSpec((None, None, W_A, PAGE_SIZE),
                                   lambda p, i, pt: (pt[seq_k(p, i), chunk(p, i) * u + j], layer, 0, 0))
    vpage = lambda j: pl.BlockSpec((None, None, W_A, PAGE_SIZE),
                                   lambda p, i, pt: (pt[seq_v(p, i), chunk(p, i) * u + j], layer, 0, 0))
    smem = pl.BlockSpec(memory_space=pltpu.SMEM)
    ya, probs, stats, ys = pl.pallas_call(
        functools.partial(_moba_fused_kernel, nb=nb, u=u, nchunk=nchunk, ds=ds),
        grid_spec=pltpu.PrefetchScalarGridSpec(
            num_scalar_prefetch=1, grid=(H_A // 2, nb),
            in_specs=[smem, const((blk, blk)), const((blk, blk)),
                      pl.BlockSpec((128, blk), lambda p, i, pt: (p, i)),
                      pl.BlockSpec((nb, blk, 128), lambda p, i, pt: (0, 0, p)),
                      pl.BlockSpec((nb, 2 * VROWS, blk), lambda p, i, pt: (0, p, 0)),
                      pl.BlockSpec((nb, 128), lambda p, i, pt: (0, p)),
                      per_k(ds, W_A), per_k(ds, W_A), per_v(ds, W_A), const((nrow, REL_BUCKETS)),
                      const((nrow, blk))]
                     + [kpage(j) for j in range(u)] + [vpage(j) for j in range(u)],
            out_specs=[pl.BlockSpec((blk, 128), lambda p, i, pt: (i, p)),
                       pl.BlockSpec((None, nchunk, nrow, u * PAGE_SIZE), lambda p, i, pt: (seq_k(p, i), 0, 0, 0)),
                       per_k(nrow, 128), per_v(ds, W_A)],
            scratch_shapes=[pltpu.VMEM((2, blk, blk), F32), pltpu.VMEM((2, blk, blk), F32),
                            pltpu.VMEM((2, nb, blk), F32), pltpu.VMEM((2, 1, blk), F32),
                            pltpu.VMEM((2, VROWS, blk), F32),
                            pltpu.VMEM((2, blk, blk), F32), pltpu.VMEM((2, blk, blk), F32),
                            pltpu.VMEM((nrow, W_A), BF16), pltpu.VMEM((nchunk, nrow, u * PAGE_SIZE), F32),
                            pltpu.VMEM((W_A, 128), F32),
                            pltpu.VMEM((2, nchunk, nrow, u * PAGE_SIZE), BF16), pltpu.VMEM((2, nrow, 128), F32),
                            pltpu.VMEM((nrow, W_A), F32)]),
        out_shape=[jax.ShapeDtypeStruct((t, W_A), F32),
                   jax.ShapeDtypeStruct((bs, nchunk, nrow, u * PAGE_SIZE), BF16),
                   jax.ShapeDtypeStruct((bs, nrow, 128), F32),
                   jax.ShapeDtypeStruct((bs - 1, ds, W_A), F32)],
        compiler_params=_cparams(("arbitrary", "arbitrary")),
        name="moba_fused",
    )(page_table, rel_bias, jnp.asarray(bk_own), jnp.asarray(bk_prev), qt, kb.reshape(nb, blk, W_A), vt,
      kmean.reshape(nb, W_A), q, k_new, v_new, rbr, bk_last, *([ck] * u), *([cv] * u))
    y_last = _moba_sample_values(layer, page_table[bs - 1:], cv, probs[bs - 1:], stats[bs - 1:], v_new[bs - 1:], u)
    return ya, jnp.concatenate([ys, y_last], axis=0)


def _mix_kernel(x_ref, ya_ref, glu_ref, uc_ref, go_ref, hb_ref, hc_ref, cbw_ref, cbb_ref, lbg_ref, lbb_ref,
                ccw_ref, wo_ref, g1_ref, b1_ref, o_ref, extb, extc, shb, *, tm, shift, hbb, hbc, alpha):
    @pl.when(pl.program_id(0) == 0)
    def _():
        extb[0:hbb, :] = hb_ref[...]
        extc[0:hbc, :] = hc_ref[...]

    mix_a = _dot(ya_ref[...].astype(BF16), wo_ref[0:W_A, :])
    extb[hbb:hbb + tm, :] = glu_ref[...]
    extc[hbc:hbc + tm, :] = uc_ref[...]
    offb = hbb - (CONV_B - 1) * shift
    if shift != 1:
        tap = lambda w: extb[offb + w * shift:offb + w * shift + tm, :]
    else:
        for r in range(SUBLANES):
            rows = tm + SUBLANES * ((CONV_B - 1 - r) // SUBLANES)
            shb[r, 0:rows, :] = extb[offb + r:offb + r + rows, :]
        tap = lambda w: shb[w % SUBLANES, w - w % SUBLANES:w - w % SUBLANES + tm, :]
    cb = jnp.broadcast_to(cbb_ref[...], (tm, W_B))
    for w in range(CONV_B):
        cb = cb + tap(w) * cbw_ref[w:w + 1, :]
    yb = _layer_norm(cb, lbg_ref[...], lbb_ref[...])
    yb = yb * _sigmoid(yb)
    offc = hbc - (CONV_C - 1) * shift
    cc = extc[offc:offc + tm, :] * ccw_ref[0:1, :]
    for w in range(1, CONV_C):
        cc = cc + extc[offc + w * shift:offc + w * shift + tm, :] * ccw_ref[w:w + 1, :]
    yc = go_ref[...] * cc
    mix = (mix_a
           + _dot(yb.astype(BF16), wo_ref[W_A:W_A + W_B, :])
           + _dot(yc.astype(BF16), wo_ref[W_A + W_B:W_A + W_B + W_C, :]))
    o_ref[...] = _layer_norm(alpha * x_ref[...] + mix, g1_ref[...], b1_ref[...])
    nb_rows = extb[tm:tm + hbb, :]
    nc_rows = extc[tm:tm + hbc, :]
    extb[0:hbb, :] = nb_rows
    extc[0:hbc, :] = nc_rows


def _mix(x, ya, glu, uc, go, hist_b, hist_c, lw, *, shift, alpha):
    t = x.shape[0]
    tm = min(512, t)
    assert t % tm == 0
    hbb, hbc = hist_b.shape[0], hist_c.shape[0]
    row = lambda n: pl.BlockSpec((tm, n), lambda i: (i, 0))
    return pl.pallas_call(
        functools.partial(_mix_kernel, tm=tm, shift=shift, hbb=hbb, hbc=hbc, alpha=alpha),
        grid=(t // tm,),
        in_specs=[row(D_MODEL), row(W_A), row(W_B), row(W_C), row(W_C), _full((hbb, W_B)), _full((hbc, W_C)),
                  _full((CONV_B, W_B)), _full((1, W_B)), _full((1, W_B)), _full((1, W_B)), _full((CONV_C, W_C)),
                  _full((D_MODEL, D_MODEL)), _full((1, D_MODEL)), _full((1, D_MODEL))],
        out_specs=row(D_MODEL),
        out_shape=jax.ShapeDtypeStruct((t, D_MODEL), F32),
        scratch_shapes=[pltpu.VMEM((hbb + tm, W_B), F32), pltpu.VMEM((hbc + tm, W_C), F32),
                        pltpu.VMEM((SUBLANES, tm + SUBLANES * ((CONV_B - 1) // SUBLANES), W_B), F32)],
        compiler_params=_cparams(("arbitrary",)),
        name="mix_ln1",
    )(x, ya, glu, uc, go, hist_b, hist_c, lw['convb_w'], lw['convb_b'], lw['lnb_g'], lw['lnb_b'],
      lw['convc_w'], lw['w_out'], lw['ln1_g'], lw['ln1_b'])


def _cross_kernel(x_ref, mk_ref, mv_ref, wq_ref, wo_ref, g_ref, b_ref, o_ref, qx_ref, acc_ref,
                  *, tm, n_groups, n_mem, interleaved, alpha):
    g = pl.program_id(1)

    @pl.when(g == 0)
    def _():
        qx_ref[...] = (_dot(x_ref[...].astype(BF16), wq_ref[...]) * (HD_X ** -0.5)).astype(BF16)
        acc_ref[...] = jnp.zeros_like(acc_ref)

    def head_mem(ref, h):
        if not interleaved:
            return ref[:, h * HD_X:(h + 1) * HD_X].astype(BF16)
        tiles = [ref[pl.ds(dt * H_X + h, n_mem, stride=MEM_ROWS), :] for dt in range(HD_X // LANES)]
        return jnp.concatenate(tiles, axis=1).astype(BF16)

    heads = []
    head_cols = [slice(h * HD_X, (h + 1) * HD_X) for h in range(H_X)]
    scores = [_dot_nt(qx_ref[:, head_cols[h]], head_mem(mk_ref, h)) for h in range(H_X)]
    for h, s in enumerate(scores):
        s = s - jnp.max(s, axis=-1, keepdims=True)
        e = jnp.exp(s)
        pr = e / jnp.sum(e, axis=-1, keepdims=True)
        heads.append(_dot(pr.astype(BF16), head_mem(mv_ref, h)))
    o = jnp.concatenate(heads, axis=-1)
    if n_groups == 1:
        acc_ref[...] = o
    else:
        rgrp = (pl.program_id(0) * tm + lax.broadcasted_iota(jnp.int32, (tm, 1), 0)) % n_groups
        acc_ref[...] = jnp.where(rgrp == g, o, acc_ref[...])

    @pl.when(g == n_groups - 1)
    def _():
        ox = _dot(acc_ref[...].astype(BF16), wo_ref[...])
        o_ref[...] = _layer_norm(alpha * x_ref[...] + ox, g_ref[...], b_ref[...])


def _interleave_mem(cache_mem):
    depth, b, n_mem = cache_mem.shape[:3]
    v = cache_mem.reshape(depth, b, n_mem, H_X, HD_X // LANES, LANES).transpose(0, 1, 2, 4, 3, 5)
    return v.reshape(depth, b, n_mem * MEM_ROWS, LANES)


def _cross(x, mem_k, mem_v, lw, *, alpha, layer=None):
    t = x.shape[0]
    tm = min(512, t)
    assert t % tm == 0
    row = pl.BlockSpec((tm, D_MODEL), lambda i, g: (i, 0))
    interleaved = layer is not None
    if interleaved:
        n_groups, n_mem = mem_k.shape[1], mem_k.shape[2] // MEM_ROWS
        mem = pl.BlockSpec((None, None, n_mem * MEM_ROWS, LANES), lambda i, g: (layer, g, 0, 0))
    else:
        n_groups, n_mem, _ = mem_k.shape
        mem = pl.BlockSpec((None, n_mem, D_MODEL), lambda i, g: (g, 0, 0))
    cst = lambda r, n: pl.BlockSpec((r, n), lambda i, g: (0, 0))
    return pl.pallas_call(
        functools.partial(_cross_kernel, tm=tm, n_groups=n_groups, n_mem=n_mem, interleaved=interleaved,
                          alpha=alpha),
        grid=(t // tm, n_groups),
        in_specs=[row, mem, mem, cst(D_MODEL, D_MODEL), cst(D_MODEL, D_MODEL), cst(1, D_MODEL), cst(1, D_MODEL)],
        out_specs=row,
        out_shape=jax.ShapeDtypeStruct((t, D_MODEL), F32),
        scratch_shapes=[pltpu.VMEM((tm, D_MODEL), BF16), pltpu.VMEM((tm, D_MODEL), F32)],
        compiler_params=_cparams(("arbitrary", "arbitrary")),
        name="cross_ln2",
    )(x, mem_k, mem_v, lw['w_qx'], lw['w_ox'], lw['ln2_g'], lw['ln2_b'])


def _ffn_kernel(x_ref, hf_ref, wg_ref, wu_ref, cfw_ref, cfb_ref, wd_ref, g_ref, b_ref, o_ref, st_ref,
                carry, ext, acc_ref, *, tm, shift, hbf, alpha):
    @pl.when(pl.program_id(0) == 0)
    def _():
        carry[...] = hf_ref[...]

    xb = x_ref[...].astype(BF16)
    off = hbf - (CONV_F - 1) * shift
    n_chunks = D_FF // FF_CHUNK

    def gate_up(c):
        cols = slice(c * FF_CHUNK, (c + 1) * FF_CHUNK)
        return _dot(xb, wg_ref[:, cols]), _dot(xb, wu_ref[:, cols])

    nxt = gate_up(0)
    for c in range(n_chunks):
        cols = slice(c * FF_CHUNK, (c + 1) * FF_CHUNK)
        gate, up = nxt
        if c + 1 < n_chunks:
            nxt = gate_up(c + 1)
        ext[0:hbf, :] = carry[:, cols]
        ext[hbf:hbf + tm, :] = gate
        gf = jnp.broadcast_to(cfb_ref[:, cols], (tm, FF_CHUNK))
        for w in range(CONV_F):
            gf = gf + ext[off + w * shift:off + w * shift + tm, :] * cfw_ref[w:w + 1, cols]
        carry[:, cols] = ext[tm:tm + hbf, :]
        hid = gf * _sigmoid(gf) * up
        part = _dot(hid.astype(BF16), wd_ref[cols, :])
        if c == 0:
            acc_ref[...] = part
        else:
            acc_ref[...] += part
    o_ref[...] = _layer_norm(alpha * x_ref[...] + acc_ref[...], g_ref[...], b_ref[...])
    st_ref[...] = carry[...]


def _ffn(x, hist_f, lw, *, shift, alpha):
    t = x.shape[0]
    tm = min(512, t)
    assert t % tm == 0
    hbf = hist_f.shape[0]
    row = pl.BlockSpec((tm, D_MODEL), lambda i: (i, 0))
    return pl.pallas_call(
        functools.partial(_ffn_kernel, tm=tm, shift=shift, hbf=hbf, alpha=alpha),
        grid=(t // tm,),
        in_specs=[row, _full((hbf, D_FF)), _full((D_MODEL, D_FF)), _full((D_MODEL, D_FF)),
                  _full((CONV_F, D_FF)), _full((1, D_FF)), _full((D_FF, D_MODEL)), _full((1, D_MODEL)),
                  _full((1, D_MODEL))],
        out_specs=[row, _full((hbf, D_FF))],
        out_shape=[jax.ShapeDtypeStruct((t, D_MODEL), F32), jax.ShapeDtypeStruct((hbf, D_FF), F32)],
        scratch_shapes=[pltpu.VMEM((hbf, D_FF), F32), pltpu.VMEM((hbf + tm, FF_CHUNK), F32),
                        pltpu.VMEM((tm, D_MODEL), F32)],
        compiler_params=_cparams(("arbitrary",)),
        name="ffn_ln3",
    )(x, hist_f, lw['w_gate'], lw['w_up'], lw['convf_w'], lw['convf_b'], lw['w_down'], lw['ln3_g'], lw['ln3_b'])


def _mem_proj_kernel(m_ref, wk_ref, wv_ref, k_ref, v_ref):
    mb = m_ref[...].astype(BF16)
    k_ref[...] = _dot(mb, wk_ref[...])
    v_ref[...] = _dot(mb, wv_ref[...])


def _mem_proj(mem, wk_bf, wv_bf):
    n = mem.shape[0]
    sd = jax.ShapeDtypeStruct((n, D_MODEL), F32)
    return pl.pallas_call(
        _mem_proj_kernel, out_shape=[sd, sd],
        compiler_params=pltpu.CompilerParams(vmem_limit_bytes=V7X_VMEM_LIMIT),
        name="mem_proj",
    )(mem, wk_bf, wv_bf)


def _pad_front(rows, n, width):
    return jnp.concatenate([jnp.zeros((n - rows.shape[0], width), F32), rows], axis=0)


def kernel(x_prompt, x_sample, mem_prompt, cache_k, cache_v, page_table, cache_mem_k, cache_mem_v, state_conv_b, state_conv_c, state_conv_f, rel_bias, w_in, w_out, convb_w, convb_b, lnb_g, lnb_b, convc_w, ln1_g, ln1_b, w_qx, w_kx, w_vx, w_ox, ln2_g, ln2_b, w_gate, w_up, convf_w, convf_b, w_down, ln3_g, ln3_b):
    bp, seq, _ = x_prompt.shape
    bs, ds, _ = x_sample.shape
    depth = w_in.shape[0]
    n_mem = mem_prompt.shape[1]
    alpha = _alpha(depth)
    assert bp == 1 and seq % MOBA_BLOCK == 0

    xp = x_prompt.reshape(seq, D_MODEL)
    xs = x_sample.transpose(1, 0, 2).reshape(ds * bs, D_MODEL)
    mem_p = mem_prompt.reshape(n_mem, D_MODEL)
    zeros_b = jnp.zeros((32, W_B), F32)
    zeros_c = jnp.zeros((8, W_C), F32)
    zeros_f = jnp.zeros((8, D_FF), F32)
    mem_k_sample = _interleave_mem(cache_mem_k)
    mem_v_sample = _interleave_mem(cache_mem_v)

    outs = {n: [] for n in ('kp', 'vp', 'ks', 'vs', 'mkp', 'mvp', 'cbp', 'cbs', 'ccp', 'ccs', 'cfp', 'cfs')}
    for l in range(depth):
        vec = lambda a: a[l].reshape(1, -1)
        lw = {'w_out': w_out[l].astype(BF16), 'convb_w': convb_w[l], 'convb_b': vec(convb_b),
              'lnb_g': vec(lnb_g), 'lnb_b': vec(lnb_b), 'convc_w': convc_w[l], 'ln1_g': vec(ln1_g),
              'ln1_b': vec(ln1_b), 'w_qx': w_qx[l].astype(BF16), 'w_ox': w_ox[l].astype(BF16),
              'ln2_g': vec(ln2_g), 'ln2_b': vec(ln2_b), 'w_gate': w_gate[l].astype(BF16),
              'w_up': w_up[l].astype(BF16), 'convf_w': convf_w[l], 'convf_b': vec(convf_b),
              'w_down': w_down[l].astype(BF16), 'ln3_g': vec(ln3_g), 'ln3_b': vec(ln3_b)}
        w_in_bf = w_in[l].astype(BF16)

        qt, kb, vt, k, v, glu, uc, go, kmean = _in_proj(xp, w_in_bf, prompt=True)
        q_s, k_s, v_s, glu_s, uc_s, go_s = _in_proj(xs, w_in_bf, prompt=False)
        seq_major = lambda a: a.reshape(ds, bs, -1).transpose(1, 0, 2)
        q_sm, k_sm, v_sm = seq_major(q_s), seq_major(k_s), seq_major(v_s)
        pages_per_step = _fused_pages_per_step((H_A // 2) * (seq // MOBA_BLOCK), bs, page_table.shape[1])
        if pages_per_step is None:
            ya = _moba_prompt(rel_bias, qt, kb, vt, kmean)
            ya_s = _moba_sample(l, rel_bias, page_table, cache_k, cache_v, q_sm, k_sm, v_sm)
        else:
            ya, ya_s = _moba_fused(l, rel_bias, qt, kb, vt, kmean, page_table, cache_k, cache_v,
                                   q_sm, k_sm, v_sm, pages_per_step)

        mk, mv = _mem_proj(mem_p, w_kx[l].astype(BF16), w_vx[l].astype(BF16))
        x1 = _mix(xp, ya, glu, uc, go, zeros_b, zeros_c, lw, shift=1, alpha=alpha)
        x2 = _cross(x1, mk[None], mv[None], lw, alpha=alpha)
        xp, st_f = _ffn(x2, zeros_f, lw, shift=1, alpha=alpha)
        from_rows = lambda a: a.reshape(1, H_A, HD_A, seq).transpose(0, 3, 1, 2)
        outs['kp'].append(from_rows(k))
        outs['vp'].append(from_rows(v))
        outs['mkp'].append(mk.reshape(1, n_mem, H_X, HD_X))
        outs['mvp'].append(mv.reshape(1, n_mem, H_X, HD_X))
        outs['cbp'].append(glu[None, seq - (CONV_B - 1):])
        outs['ccp'].append(uc[None, seq - (CONV_C - 1):])
        outs['cfp'].append(st_f[None, st_f.shape[0] - (CONV_F - 1):])

        hb = state_conv_b[l].transpose(1, 0, 2)
        hc = state_conv_c[l].transpose(1, 0, 2)
        hf = state_conv_f[l].transpose(1, 0, 2)
        glu, uc, go = glu_s, uc_s, go_s
        ya = ya_s.transpose(1, 0, 2).reshape(ds * bs, W_A)
        x1 = _mix(xs, ya, glu, uc, go, hb.reshape(-1, W_B), hc.reshape(-1, W_C), lw, shift=bs, alpha=alpha)
        x2 = _cross(x1, mem_k_sample, mem_v_sample, lw, alpha=alpha, layer=l)
        xs, st_f = _ffn(x2, hf.reshape(-1, D_FF), lw, shift=bs, alpha=alpha)
        outs['ks'].append(k_sm.reshape(bs, ds, H_A, HD_A))
        outs['vs'].append(v_sm.reshape(bs, ds, H_A, HD_A))
        tail = lambda hist, new, n: jnp.concatenate([hist, new.reshape(ds, bs, -1)], axis=0)[-n:].transpose(1, 0, 2)
        outs['cbs'].append(tail(hb, glu, CONV_B - 1))
        outs['ccs'].append(tail(hc, uc, CONV_C - 1))
        outs['cfs'].append(st_f.reshape(CONV_F - 1, bs, D_FF).transpose(1, 0, 2))

    y_prompt = xp.reshape(1, seq, D_MODEL)
    y_sample = xs.reshape(ds, bs, D_MODEL).transpose(1, 0, 2)
    st = lambda n, axis: jnp.stack(outs[n], axis=axis)
    return (y_prompt, y_sample, st('kp', 1), st('vp', 1), st('ks', 1), st('vs', 1), st('mkp', 0), st('mvp', 0),
            st('cbp', 0), st('cbs', 0), st('ccp', 0), st('ccs', 0), st('cfp', 0), st('cfs', 0))
```

```python
import functools
import math

import numpy as np
import jax
import jax.numpy as jnp
from jax import lax
from jax.experimental import pallas as pl
from jax.experimental.pallas import tpu as pltpu

F32 = jnp.float32
BF16 = jnp.bfloat16

D_MODEL = 1024
PAGE_SIZE = 128
HD_A = 64
W_A = 512
H_A = 8
W_B = 256
W_C = 256
PROJ_IN = 2816
MOBA_BLOCK = 256
MOBA_TOPK = 3
CONV_B = 31
CONV_C = 3
CONV_F = 3
D_FF = 2816
FF_CHUNK = 256
H_X = 4
HD_X = 256
REL_BUCKETS = 32
REL_MAX_DIST = 128
LN_EPS = 1e-5
SUBLANES = 8
LANES = 128
MEM_ROWS = H_X * (HD_X // LANES)

NEG = -1e30
VROWS = 80
FAR_UNROLL = 8
FAR_TAIL_UNROLL = 2
LOG2E = math.log2(math.e)
V7X_VMEM_LIMIT = 56 * 1024 * 1024
ROW_TILE = 512
SAMPLE_PAGES_PER_STEP = 32


def _alpha(depth):
    return (2 * depth) ** 0.25


def _row_tile(t):
    tm = min(ROW_TILE, t)
    assert t % tm == 0 and (tm % SUBLANES == 0 or tm == t)
    return tm


def _cparams(sem):
    return pltpu.CompilerParams(dimension_semantics=sem, vmem_limit_bytes=V7X_VMEM_LIMIT)


def _full(shape):
    return pl.BlockSpec(shape, lambda *_: (0,) * len(shape))


def _sigmoid(x):
    return 1.0 / (1.0 + jnp.exp(-x))


def _layer_norm(h, g, b):
    mu = jnp.mean(h, axis=-1, keepdims=True)
    d = h - mu
    var = jnp.mean(d * d, axis=-1, keepdims=True)
    return d * lax.rsqrt(var + LN_EPS) * g + b


def _dot(a, b):
    return jnp.dot(a, b, preferred_element_type=F32)


def _dot_nt(a, b):
    return lax.dot_general(a, b, (((1,), (1,)), ((), ())), preferred_element_type=F32)


def _rel_bucket_np(dist):
    n = np.maximum(dist, 0)
    max_exact = REL_BUCKETS // 2
    nf = np.maximum(n, max_exact).astype(np.float32)
    large = max_exact + (np.log(nf / np.float32(max_exact)) / np.float32(math.log(REL_MAX_DIST / max_exact))
                         * np.float32(REL_BUCKETS - max_exact)).astype(np.int32)
    return np.where(n < max_exact, n, np.minimum(large, REL_BUCKETS - 1)).astype(np.int32)


def _in_proj_kernel(x_ref, w_ref, *outs, prompt, tm):
    xb = x_ref[...].astype(BF16)

    def proj(lo, hi):
        return _dot(xb, w_ref[:, lo:hi])

    q = proj(0, W_A) * (HD_A ** -0.5 * (LOG2E if prompt else 1.0))
    k = proj(W_A, 2 * W_A)
    v = proj(2 * W_A, 3 * W_A)
    o = 3 * W_A
    glu = proj(o, o + W_B) * _sigmoid(proj(o + W_B, o + 2 * W_B))
    o += 2 * W_B
    gate_out = proj(o, o + W_C)
    uc = proj(o + W_C, o + 2 * W_C) * proj(o + 2 * W_C, o + 3 * W_C)
    if prompt:
        qt_ref, kb_ref, vt_ref, k_ref, v_ref, glu_ref, uc_ref, go_ref, km_ref = outs
        for c in range(W_A // LANES):
            cols = slice(c * LANES, (c + 1) * LANES)
            qt_ref[cols, :] = q[:, cols].T.astype(BF16)
            k_ref[cols, :] = k[:, cols].T
            v_ref[cols, :] = v[:, cols].T
        kb_ref[...] = k.astype(BF16)
        for s in range(tm // MOBA_BLOCK):
            rows = slice(s * MOBA_BLOCK, (s + 1) * MOBA_BLOCK)
            for c in range(W_A // LANES):
                vt2 = v[rows, c * LANES:(c + 1) * LANES].T.astype(BF16)
                for hh in range(2):
                    r0 = (2 * c + hh) * VROWS
                    vt_ref[s, r0:r0 + HD_A, :] = vt2[hh * HD_A:(hh + 1) * HD_A, :]
                    vt_ref[s, r0 + HD_A:r0 + VROWS, :] = jnp.ones((VROWS - HD_A, MOBA_BLOCK), BF16)
            km_ref[s] = jnp.sum(k[rows, :], axis=0, keepdims=True) * (1.0 / MOBA_BLOCK)
    else:
        q_ref, k_ref, v_ref, glu_ref, uc_ref, go_ref = outs
        q_ref[...] = q
        k_ref[...] = k
        v_ref[...] = v
    glu_ref[...] = glu
    uc_ref[...] = uc
    go_ref[...] = gate_out


def _in_proj(x, w_bf, *, prompt):
    t = x.shape[0]
    tm = _row_tile(t)
    nb = t // MOBA_BLOCK
    row = lambda n: pl.BlockSpec((tm, n), lambda i: (i, 0))
    sd = jax.ShapeDtypeStruct
    if prompt:
        assert tm % MOBA_BLOCK == 0
        spb = tm // MOBA_BLOCK
        out_shape = [sd((W_A, t), BF16), sd((t, W_A), BF16), sd((nb, H_A * VROWS, MOBA_BLOCK), BF16),
                     sd((W_A, t), F32), sd((W_A, t), F32), sd((t, W_B), F32), sd((t, W_C), F32),
                     sd((t, W_C), F32), sd((nb, 1, W_A), F32)]
        col = pl.BlockSpec((W_A, tm), lambda i: (0, i))
        out_specs = [col, row(W_A),
                     pl.BlockSpec((spb, H_A * VROWS, MOBA_BLOCK), lambda i: (i, 0, 0)),
                     col, col, row(W_B), row(W_C), row(W_C),
                     pl.BlockSpec((spb, 1, W_A), lambda i: (i, 0, 0))]
    else:
        out_shape = [sd((t, W_A), F32), sd((t, W_A), F32), sd((t, W_A), F32), sd((t, W_B), F32),
                     sd((t, W_C), F32), sd((t, W_C), F32)]
        out_specs = [row(W_A), row(W_A), row(W_A), row(W_B), row(W_C), row(W_C)]
    return pl.pallas_call(
        functools.partial(_in_proj_kernel, prompt=prompt, tm=tm),
        grid=(t // tm,),
        in_specs=[row(D_MODEL), _full((D_MODEL, PROJ_IN))],
        out_specs=out_specs, out_shape=out_shape,
        compiler_params=_cparams(("arbitrary",)),
        name="in_proj_prompt" if prompt else "in_proj_sample",
    )(x, w_bf)


def _moba_prompt_kernel(*refs, nb):
    _moba_prompt_step(pl.program_id(0), pl.program_id(1), *refs, nb=nb)


def _moba_prompt_step(p, i, rb_ref, bko_ref, bkp_ref, qt_ref, k_ref, vt_ref, km_ref, o_ref,
                      bown, bprev, sel_ref, m_ref, acc_ref, sping, spong, *, nb):
    blk = MOBA_BLOCK

    @pl.when(i == 0)
    def _():
        bko = bko_ref[...]
        bkp = bkp_ref[...]
        for hh in range(2):
            h = 2 * p + hh
            far = rb_ref[REL_BUCKETS - 1, h]
            bo = jnp.full((blk, blk), NEG, F32)
            bp = jnp.zeros((blk, blk), F32)
            for b in range(REL_BUCKETS):
                val = (rb_ref[b, h] - far) * LOG2E
                bo = jnp.where(bko == b, val, bo)
                bp = jnp.where(bkp == b, val, bp)
            bown[hh] = bo
            bprev[hh] = bp

    qt = qt_ref[...]
    rowi = lax.broadcasted_iota(jnp.int32, qt.shape, 0)
    zero = jnp.zeros_like(qt)
    qts = [jnp.where(rowi < HD_A, qt, zero), jnp.where(rowi >= HD_A, qt, zero)]

    km = km_ref[...].astype(BF16)
    bidx = lax.broadcasted_iota(jnp.int32, (nb, blk), 0)
    for hh in range(2):
        g = _dot(km, qts[hh])
        g = jnp.where(bidx < i, g, -jnp.inf)
        selected = jnp.zeros((nb, blk), jnp.bool_)
        for _ in range(MOBA_TOPK):
            mx = jnp.max(g, axis=0, keepdims=True)
            first = jnp.min(jnp.where(g == mx, bidx, nb), axis=0, keepdims=True)
            hit = (bidx == first) & (mx > -jnp.inf)
            selected = selected | hit
            g = jnp.where(hit, -jnp.inf, g)
        sel_ref[hh] = jnp.where(selected, 0.0, NEG)
        m_ref[hh] = jnp.full((1, blk), NEG, F32)
        acc_ref[hh] = jnp.zeros((VROWS, blk), F32)

    def scores(j, hh):
        return _dot(k_ref[j], qts[hh])

    def softmax_pv(s, j, hh, colbias):
        mx = jnp.max(s, axis=0, keepdims=True)
        if colbias is not None:
            mx = mx + colbias
        m_old = m_ref[hh]
        m_new = jnp.maximum(m_old, mx)
        a = jnp.exp2(m_old - m_new)
        c = m_new if colbias is None else m_new - colbias
        pt = jnp.exp2(s - c).astype(BF16)
        r = _dot(vt_ref[j, hh * VROWS:(hh + 1) * VROWS, :], pt)
        acc_ref[hh] = a * acc_ref[hh] + r
        m_ref[hh] = m_new

    n_far = jnp.maximum(i - 1, 0)
    jp = jnp.maximum(i - 1, 0)
    s_own = [scores(i, hh) + bown[hh] for hh in range(2)]
    s_prev = [scores(jp, hh) + bprev[hh] for hh in range(2)]
    for hh in range(2):
        sping[hh] = scores(0, hh)
    for hh in range(2):
        softmax_pv(s_own[hh], i, hh, None)
    for hh in range(2):
        prev_mask = jnp.where(i >= 1, sel_ref[hh, pl.ds(jp, 1), :], NEG)
        softmax_pv(s_prev[hh], jp, hh, prev_mask)

    bufs = [sping, spong]

    def far_loop(first, unroll, trips):
        last = n_far - 1

        def trip(t, carry):
            for u in range(unroll):
                cur, nxt = bufs[u % 2], bufs[1 - u % 2]
                j = first + t * unroll + u
                jn = jnp.minimum(j + 1, last)
                for hh in range(2):
                    nxt[hh] = scores(jn, hh)
                jc = jnp.minimum(j, last)
                for hh in range(2):
                    mask = jnp.where(j <= last, sel_ref[hh, pl.ds(jc, 1), :], NEG)
                    softmax_pv(cur[hh], jc, hh, mask)
            return carry

        lax.fori_loop(0, trips, trip, 0)

    main = n_far // FAR_UNROLL * FAR_UNROLL
    far_loop(0, FAR_UNROLL, n_far // FAR_UNROLL)
    far_loop(main, FAR_TAIL_UNROLL, (n_far - main + FAR_TAIL_UNROLL - 1) // FAR_TAIL_UNROLL)

    outs = []
    for hh in range(2):
        acc = acc_ref[hh]
        outs.append(acc[0:HD_A, :] / acc[HD_A:HD_A + 1, :])
    o_ref[...] = jnp.concatenate(outs, axis=0).T


def _moba_prompt(rel_bias, qt, kb, vt, kmean):
    t = qt.shape[1]
    nb = t // MOBA_BLOCK
    blk = MOBA_BLOCK
    c = np.arange(blk)[:, None]
    r = np.arange(blk)[None, :]
    bk_own = np.where(c <= r, _rel_bucket_np(r - c), -1).astype(np.int32)
    bk_prev = _rel_bucket_np(blk + r - c)
    smem = pl.BlockSpec(memory_space=pltpu.SMEM)
    return pl.pallas_call(
        functools.partial(_moba_prompt_kernel, nb=nb),
        grid=(H_A // 2, nb),
        in_specs=[smem, _full((blk, blk)), _full((blk, blk)),
                  pl.BlockSpec((LANES, blk), lambda p, i: (p, i)),
                  pl.BlockSpec((nb, blk, LANES), lambda p, i: (0, 0, p)),
                  pl.BlockSpec((nb, 2 * VROWS, blk), lambda p, i: (0, p, 0)),
                  pl.BlockSpec((nb, LANES), lambda p, i: (0, p))],
        out_specs=pl.BlockSpec((blk, LANES), lambda p, i: (i, p)),
        out_shape=jax.ShapeDtypeStruct((t, W_A), F32),
        scratch_shapes=[pltpu.VMEM((2, blk, blk), F32), pltpu.VMEM((2, blk, blk), F32),
                        pltpu.VMEM((2, nb, blk), F32), pltpu.VMEM((2, 1, blk), F32),
                        pltpu.VMEM((2, VROWS, blk), F32),
                        pltpu.VMEM((2, blk, blk), F32), pltpu.VMEM((2, blk, blk), F32)],
        compiler_params=_cparams(("arbitrary", "arbitrary")),
        name="moba_prompt",
    )(rel_bias, jnp.asarray(bk_own), jnp.asarray(bk_prev), qt, kb.reshape(nb, blk, W_A), vt,
      kmean.reshape(nb, W_A))


def _head_rows(x, n_tok):
    lane_head = lax.broadcasted_iota(jnp.int32, (H_A, W_A), 1) // HD_A
    sub = lax.broadcasted_iota(jnp.int32, (H_A, W_A), 0)
    parts = [jnp.where(lane_head == sub, jnp.broadcast_to(x[t:t + 1, :], (H_A, W_A)), 0.0)
             for t in range(n_tok)]
    return jnp.concatenate(parts, axis=0)


def _moba_sample_k_kernel(pt_ref, q_ref, kn_ref, rbr_ref, bkl_ref, *rest, u, nchunk, ds):
    kpages = rest[:u]
    p_ref, st_ref = rest[u], rest[u + 1]
    qbd_ref, s_ref, ks_ref = rest[u + 2:]
    _sample_keys_step(pl.program_id(1), q_ref, kn_ref, rbr_ref, bkl_ref, kpages, [p_ref], [st_ref],
                      qbd_ref, s_ref, ks_ref, u=u, nchunk=nchunk, ds=ds)


def _sample_keys_step(c, q_ref, kn_ref, rbr_ref, bkl_ref, kpages, p_refs, st_refs, qbd_ref, s_ref, ks_ref,
                      *, u, nchunk, ds):
    nrow = ds * H_A
    bpc = u // 2
    nbp = nchunk * bpc

    @pl.when(c == 0)
    def _():
        qbd_ref[...] = _head_rows(q_ref[...], ds).astype(BF16)
        ks_ref[...] = jnp.zeros_like(ks_ref)

    qbd = qbd_ref[...]
    ks = ks_ref[...]
    lane_blk = lax.broadcasted_iota(jnp.int32, ks.shape, 1)
    for v in range(bpc):
        k0, k1 = kpages[2 * v][...], kpages[2 * v + 1][...]
        kk = jnp.concatenate([k0.astype(BF16), k1.astype(BF16)], axis=1)
        s_ref[c, :, v * MOBA_BLOCK:(v + 1) * MOBA_BLOCK] = _dot(qbd, kk)
        col = jnp.sum(k0 + k1, axis=1, keepdims=True)
        ks = jnp.where(lane_blk == c * bpc + v, col, ks)
    ks_ref[...] = ks

    @pl.when(c == nchunk - 1)
    def _():
        blk = MOBA_BLOCK
        km = (ks_ref[...] * (1.0 / blk)).astype(BF16)
        g = _dot(qbd, km)
        lane = lax.broadcasted_iota(jnp.int32, g.shape, 1)
        g = jnp.where(lane < nbp, g, -jnp.inf)
        selected = jnp.zeros(g.shape, jnp.bool_)
        for _ in range(min(MOBA_TOPK, nbp)):
            mx = jnp.max(g, axis=1, keepdims=True)
            first = jnp.min(jnp.where(g == mx, lane, nbp), axis=1, keepdims=True)
            hit = (lane == first) & (mx > -jnp.inf)
            selected = selected | hit
            g = jnp.where(hit, -jnp.inf, g)
        maskadd = jnp.where(selected, 0.0, NEG)

        rbr = rbr_ref[...]
        far = rbr[:, REL_BUCKETS - 1:REL_BUCKETS]
        bkl = bkl_ref[...]
        bias_last = jnp.zeros((nrow, blk), F32)
        for b in range(REL_BUCKETS):
            bias_last = jnp.where(bkl == b, rbr[:, b:b + 1], bias_last)

        trow = lax.broadcasted_iota(jnp.int32, (nrow, 1), 0) // H_A
        qf = qbd.astype(F32)
        knb = kn_ref[...].astype(BF16).astype(F32)
        own = []
        for cc in range(ds):
            sc = jnp.sum(qf * knb[cc:cc + 1, :], axis=1, keepdims=True)
            bias = jnp.zeros((nrow, 1), F32)
            for d in range(ds - cc):
                bias = jnp.where(trow - cc == d, rbr[:, d:d + 1], bias)
            own.append(jnp.where(trow >= cc, sc + bias, NEG))

        def logits(n):
            cn, off = n // bpc, (n % bpc) * blk
            b = bias_last if n == nbp - 1 else far
            return s_ref[cn, :, off:off + blk] + b + maskadd[:, n:n + 1]

        mt = logits(0)
        for n in range(1, nbp):
            mt = jnp.maximum(mt, logits(n))
        m = jnp.max(mt, axis=1, keepdims=True)
        for cc in range(ds):
            m = jnp.maximum(m, own[cc])
        lt = jnp.zeros((nrow, blk), F32)
        for n in range(nbp):
            cn, off = n // bpc, (n % bpc) * blk
            pn = jnp.exp(logits(n) - m)
            lt = lt + pn
            for p_ref in p_refs:
                p_ref[cn, :, off:off + blk] = pn.astype(BF16)
        l = jnp.sum(lt, axis=1, keepdims=True)
        lane128 = lax.broadcasted_iota(jnp.int32, (nrow, LANES), 1)
        stats = jnp.zeros((nrow, LANES), F32)
        for cc in range(ds):
            po = jnp.exp(own[cc] - m)
            l = l + po
            stats = jnp.where(lane128 == cc, po, stats)
        for st_ref in st_refs:
            st_ref[...] = jnp.where(lane128 == ds, l, stats)


def _moba_sample_v_kernel(pt_ref, p_ref, st_ref, vn_ref, *rest, u, nchunk, ds):
    vpages = rest[:u]
    o_ref, acc_ref = rest[u], rest[u + 1]
    _sample_values_step(pl.program_id(1), p_ref, st_ref, vn_ref, vpages, o_ref, acc_ref, u=u, nchunk=nchunk, ds=ds)


def _sample_values_step(c, p_ref, st_ref, vn_ref, vpages, o_ref, acc_ref, *, u, nchunk, ds):

    @pl.when(c == 0)
    def _():
        acc_ref[...] = jnp.zeros_like(acc_ref)

    acc = acc_ref[...]
    for v in range(u // 2):
        vv = jnp.concatenate([vpages[2 * v][...].astype(BF16), vpages[2 * v + 1][...].astype(BF16)], axis=1)
        acc = acc + _dot_nt(p_ref[:, v * MOBA_BLOCK:(v + 1) * MOBA_BLOCK], vv)
    acc_ref[...] = acc

    @pl.when(c == nchunk - 1)
    def _():
        a = acc_ref[...]
        st = st_ref[...]
        vnb = vn_ref[...].astype(BF16).astype(F32)
        for cc in range(ds):
            a = a + st[:, cc:cc + 1].astype(BF16).astype(F32) * vnb[cc:cc + 1, :]
        a = a / st[:, ds:ds + 1]
        lane_head = lax.broadcasted_iota(jnp.int32, (H_A, W_A), 1) // HD_A
        sub = lax.broadcasted_iota(jnp.int32, (H_A, W_A), 0)
        for t in range(ds):
            rows = a[t * H_A:(t + 1) * H_A, :]
            o_ref[t:t + 1, :] = jnp.sum(jnp.where(lane_head == sub, rows, 0.0), axis=0, keepdims=True)


def _page_view(cache):
    n_pool, depth = cache.shape[:2]
    return cache.transpose(0, 1, 3, 4, 2).reshape(n_pool, depth, W_A, PAGE_SIZE)


def _sample_tables(rel_bias, ds):
    nrow = ds * H_A
    rbr = jnp.tile(rel_bias.T, (ds, 1))
    trow = np.arange(nrow)[:, None] // H_A
    bk_last = _rel_bucket_np(MOBA_BLOCK + trow - np.arange(MOBA_BLOCK)[None, :])
    return rbr, jnp.asarray(bk_last)


def _check_sample_shapes(n_pages, ds, u):
    past_len = n_pages * PAGE_SIZE
    assert past_len % MOBA_BLOCK == 0 and ds <= REL_BUCKETS // 2
    assert n_pages % u == 0 and u % 2 == 0 and past_len // MOBA_BLOCK <= LANES


def _moba_sample_values(layer, page_table, cv, probs, stats, v_new, u):
    b, ds, _ = v_new.shape
    nchunk = page_table.shape[1] // u
    nrow = ds * H_A
    page_spec = lambda j: pl.BlockSpec((None, None, W_A, PAGE_SIZE),
                                       lambda bi, c, pt: (pt[bi, c * u + j], layer, 0, 0))
    per_seq = lambda r, n: pl.BlockSpec((None, r, n), lambda bi, c, pt: (bi, 0, 0))
    return pl.pallas_call(
        functools.partial(_moba_sample_v_kernel, u=u, nchunk=nchunk, ds=ds),
        grid_spec=pltpu.PrefetchScalarGridSpec(
            num_scalar_prefetch=1, grid=(b, nchunk),
            in_specs=[pl.BlockSpec((None, None, nrow, u * PAGE_SIZE), lambda bi, c, pt: (bi, c, 0, 0)),
                      per_seq(nrow, LANES), per_seq(ds, W_A)] + [page_spec(j) for j in range(u)],
            out_specs=per_seq(ds, W_A),
            scratch_shapes=[pltpu.VMEM((nrow, W_A), F32)]),
        out_shape=jax.ShapeDtypeStruct((b, ds, W_A), F32),
        compiler_params=_cparams(("arbitrary", "arbitrary")),
        name="moba_sample_values",
    )(page_table, probs, stats, v_new, *([cv] * u))


def _moba_sample(layer, rel_bias, page_table, cache_k, cache_v, q, k_new, v_new):
    b, ds, _ = q.shape
    n_pages = page_table.shape[1]
    u = min(SAMPLE_PAGES_PER_STEP, n_pages)
    _check_sample_shapes(n_pages, ds, u)
    nchunk = n_pages // u
    nrow = ds * H_A
    ck, cv = _page_view(cache_k), _page_view(cache_v)
    rbr, bk_last = _sample_tables(rel_bias, ds)

    def page_spec(j):
        return pl.BlockSpec((None, None, W_A, PAGE_SIZE),
                            lambda bi, c, pt: (pt[bi, c * u + j], layer, 0, 0))

    per_seq = lambda r, n: pl.BlockSpec((None, r, n), lambda bi, c, pt: (bi, 0, 0))
    const = lambda r, n: pl.BlockSpec((r, n), lambda bi, c, pt: (0, 0))
    probs, stats = pl.pallas_call(
        functools.partial(_moba_sample_k_kernel, u=u, nchunk=nchunk, ds=ds),
        grid_spec=pltpu.PrefetchScalarGridSpec(
            num_scalar_prefetch=1, grid=(b, nchunk),
            in_specs=[per_seq(ds, W_A), per_seq(ds, W_A), const(nrow, REL_BUCKETS), const(nrow, MOBA_BLOCK)]
                     + [page_spec(j) for j in range(u)],
            out_specs=[pl.BlockSpec((None, nchunk, nrow, u * PAGE_SIZE), lambda bi, c, pt: (bi, 0, 0, 0)),
                       per_seq(nrow, LANES)],
            scratch_shapes=[pltpu.VMEM((nrow, W_A), BF16), pltpu.VMEM((nchunk, nrow, u * PAGE_SIZE), F32),
                            pltpu.VMEM((W_A, LANES), F32)]),
        out_shape=[jax.ShapeDtypeStruct((b, nchunk, nrow, u * PAGE_SIZE), BF16),
                   jax.ShapeDtypeStruct((b, nrow, LANES), F32)],
        compiler_params=_cparams(("arbitrary", "arbitrary")),
        name="moba_sample_keys",
    )(page_table, q, k_new, rbr, bk_last, *([ck] * u))
    return _moba_sample_values(layer, page_table, cv, probs, stats, v_new, u)


def _moba_fused_kernel(pt_ref, rb_ref, bko_ref, bkp_ref, qt_ref, k_ref, vt_ref, km_ref,
                       q_ref, kn_ref, vn_ref, rbr_ref, bkl_ref, *rest, nb, u, nchunk, ds):
    kpages, vpages = rest[:u], rest[u:2 * u]
    o_ref, p_out, st_out, y_ref = rest[2 * u:2 * u + 4]
    (bown, bprev, sel_ref, m_ref, acc_ref, sping, spong,
     qbd_ref, s_ref, ks_ref, p_scr, st_scr, accv_ref) = rest[2 * u + 4:]
    p = pl.program_id(0)
    i = pl.program_id(1)
    step = p * nb + i
    b = step // nchunk
    c = step % nchunk
    par = b % 2
    _sample_keys_step(c, q_ref, kn_ref, rbr_ref, bkl_ref, kpages, [p_out, p_scr.at[par]],
                      [st_out, st_scr.at[par]], qbd_ref, s_ref, ks_ref, u=u, nchunk=nchunk, ds=ds)

    @pl.when(b >= 1)
    def _():
        _sample_values_step(c, p_scr.at[1 - par, c], st_scr.at[1 - par], vn_ref, vpages, y_ref, accv_ref,
                            u=u, nchunk=nchunk, ds=ds)

    _moba_prompt_step(p, i, rb_ref, bko_ref, bkp_ref, qt_ref, k_ref, vt_ref, km_ref, o_ref,
                      bown, bprev, sel_ref, m_ref, acc_ref, sping, spong, nb=nb)


def _fused_pages_per_step(n_steps, bs, n_pages):
    if bs < 2 or (bs * n_pages) % n_steps:
        return None
    u = bs * n_pages // n_steps
    if u < 2 or u % 2 or u > SAMPLE_PAGES_PER_STEP or n_pages % u:
        return None
    return u


def _moba_fused(layer, rel_bias, qt, kb, vt, kmean, page_table, cache_k, cache_v, q, k_new, v_new, u):
    t = qt.shape[1]
    nb = t // MOBA_BLOCK
    blk = MOBA_BLOCK
    bs, ds, _ = q.shape
    n_pages = page_table.shape[1]
    _check_sample_shapes(n_pages, ds, u)
    nchunk = n_pages // u
    assert bs * nchunk == (H_A // 2) * nb
    nrow = ds * H_A
    ck, cv = _page_view(cache_k), _page_view(cache_v)
    rbr, bk_last = _sample_tables(rel_bias, ds)
    c_ = np.arange(blk)[:, None]
    r_ = np.arange(blk)[None, :]
    bk_own = np.where(c_ <= r_, _rel_bucket_np(r_ - c_), -1).astype(np.int32)
    bk_prev = _rel_bucket_np(blk + r_ - c_)

    seq_k = lambda p, i: (p * nb + i) // nchunk
    seq_v = lambda p, i: jnp.maximum(seq_k(p, i) - 1, 0)
    chunk = lambda p, i: (p * nb + i) % nchunk
    const = lambda shape: pl.BlockSpec(shape, lambda p, i, pt: (0,) * len(shape))
    per_k = lambda r, n: pl.BlockSpec((None, r, n), lambda p, i, pt: (seq_k(p, i), 0, 0))
    per_v = lambda r, n: pl.BlockSpec((None, r, n), lambda p, i, pt: (seq_v(p, i), 0, 0))
    kpage = lambda j: pl.BlockSpec((None, None, W_A, PAGE_SIZE),
                                   lambda p, i, pt: (pt[seq_k(p, i), chunk(p, i) * u + j], layer, 0, 0))
    vpage = lambda j: pl.BlockSpec((None, None, W_A, PAGE_SIZE),
                                   lambda p, i, pt: (pt[seq_v(p, i), chunk(p, i) * u + j], layer, 0, 0))
    smem = pl.BlockSpec(memory_space=pltpu.SMEM)
    ya, probs, stats, ys = pl.pallas_call(
        functools.partial(_moba_fused_kernel, nb=nb, u=u, nchunk=nchunk, ds=ds),
        grid_spec=pltpu.PrefetchScalarGridSpec(
            num_scalar_prefetch=1, grid=(H_A // 2, nb),
            in_specs=[smem, const((blk, blk)), const((blk, blk)),
                      pl.BlockSpec((LANES, blk), lambda p, i, pt: (p, i)),
                      pl.BlockSpec((nb, blk, LANES), lambda p, i, pt: (0, 0, p)),
                      pl.BlockSpec((nb, 2 * VROWS, blk), lambda p, i, pt: (0, p, 0)),
                      pl.BlockSpec((nb, LANES), lambda p, i, pt: (0, p)),
                      per_k(ds, W_A), per_k(ds, W_A), per_v(ds, W_A), const((nrow, REL_BUCKETS)),
                      const((nrow, blk))]
                     + [kpage(j) for j in range(u)] + [vpage(j) for j in range(u)],
            out_specs=[pl.BlockSpec((blk, LANES), lambda p, i, pt: (i, p)),
                       pl.BlockSpec((None, nchunk, nrow, u * PAGE_SIZE), lambda p, i, pt: (seq_k(p, i), 0, 0, 0)),
                       per_k(nrow, LANES), per_v(ds, W_A)],
            scratch_shapes=[pltpu.VMEM((2, blk, blk), F32), pltpu.VMEM((2, blk, blk), F32),
                            pltpu.VMEM((2, nb, blk), F32), pltpu.VMEM((2, 1, blk), F32),
                            pltpu.VMEM((2, VROWS, blk), F32),
                            pltpu.VMEM((2, blk, blk), F32), pltpu.VMEM((2, blk, blk), F32),
                            pltpu.VMEM((nrow, W_A), BF16), pltpu.VMEM((nchunk, nrow, u * PAGE_SIZE), F32),
                            pltpu.VMEM((W_A, LANES), F32),
                            pltpu.VMEM((2, nchunk, nrow, u * PAGE_SIZE), BF16), pltpu.VMEM((2, nrow, LANES), F32),
                            pltpu.VMEM((nrow, W_A), F32)]),
        out_shape=[jax.ShapeDtypeStruct((t, W_A), F32),
                   jax.ShapeDtypeStruct((bs, nchunk, nrow, u * PAGE_SIZE), BF16),
                   jax.ShapeDtypeStruct((bs, nrow, LANES), F32),
                   jax.ShapeDtypeStruct((bs - 1, ds, W_A), F32)],
        compiler_params=_cparams(("arbitrary", "arbitrary")),
        name="moba_fused",
    )(page_table, rel_bias, jnp.asarray(bk_own), jnp.asarray(bk_prev), qt, kb.reshape(nb, blk, W_A), vt,
      kmean.reshape(nb, W_A), q, k_new, v_new, rbr, bk_last, *([ck] * u), *([cv] * u))
    y_last = _moba_sample_values(layer, page_table[bs - 1:], cv, probs[bs - 1:], stats[bs - 1:], v_new[bs - 1:], u)
    return ya, jnp.concatenate([ys, y_last], axis=0)


def _mix_kernel(x_ref, ya_ref, glu_ref, uc_ref, go_ref, hb_ref, hc_ref, cbw_ref, cbb_ref, lbg_ref, lbb_ref,
                ccw_ref, wo_ref, g1_ref, b1_ref, o_ref, extb, extc, shb, *, tm, shift, hbb, hbc, alpha):
    @pl.when(pl.program_id(0) == 0)
    def _():
        extb[0:hbb, :] = hb_ref[...]
        extc[0:hbc, :] = hc_ref[...]

    mix_a = _dot(ya_ref[...].astype(BF16), wo_ref[0:W_A, :])
    extb[hbb:hbb + tm, :] = glu_ref[...]
    extc[hbc:hbc + tm, :] = uc_ref[...]
    offb = hbb - (CONV_B - 1) * shift
    if shift != 1:
        tap = lambda w: extb[offb + w * shift:offb + w * shift + tm, :]
    else:
        for r in range(SUBLANES):
            rows = tm + SUBLANES * ((CONV_B - 1 - r) // SUBLANES)
            shb[r, 0:rows, :] = extb[offb + r:offb + r + rows, :]
        tap = lambda w: shb[w % SUBLANES, w - w % SUBLANES:w - w % SUBLANES + tm, :]
    cb = jnp.broadcast_to(cbb_ref[...], (tm, W_B))
    for w in range(CONV_B):
        cb = cb + tap(w) * cbw_ref[w:w + 1, :]
    yb = _layer_norm(cb, lbg_ref[...], lbb_ref[...])
    yb = yb * _sigmoid(yb)
    offc = hbc - (CONV_C - 1) * shift
    cc = extc[offc:offc + tm, :] * ccw_ref[0:1, :]
    for w in range(1, CONV_C):
        cc = cc + extc[offc + w * shift:offc + w * shift + tm, :] * ccw_ref[w:w + 1, :]
    yc = go_ref[...] * cc
    mix = (mix_a
           + _dot(yb.astype(BF16), wo_ref[W_A:W_A + W_B, :])
           + _dot(yc.astype(BF16), wo_ref[W_A + W_B:W_A + W_B + W_C, :]))
    o_ref[...] = _layer_norm(alpha * x_ref[...] + mix, g1_ref[...], b1_ref[...])
    nb_rows = extb[tm:tm + hbb, :]
    nc_rows = extc[tm:tm + hbc, :]
    extb[0:hbb, :] = nb_rows
    extc[0:hbc, :] = nc_rows


def _mix(x, ya, glu, uc, go, hist_b, hist_c, lw, *, shift, alpha):
    t = x.shape[0]
    tm = _row_tile(t)
    hbb, hbc = hist_b.shape[0], hist_c.shape[0]
    row = lambda n: pl.BlockSpec((tm, n), lambda i: (i, 0))
    return pl.pallas_call(
        functools.partial(_mix_kernel, tm=tm, shift=shift, hbb=hbb, hbc=hbc, alpha=alpha),
        grid=(t // tm,),
        in_specs=[row(D_MODEL), row(W_A), row(W_B), row(W_C), row(W_C), _full((hbb, W_B)), _full((hbc, W_C)),
                  _full((CONV_B, W_B)), _full((1, W_B)), _full((1, W_B)), _full((1, W_B)), _full((CONV_C, W_C)),
                  _full((D_MODEL, D_MODEL)), _full((1, D_MODEL)), _full((1, D_MODEL))],
        out_specs=row(D_MODEL),
        out_shape=jax.ShapeDtypeStruct((t, D_MODEL), F32),
        scratch_shapes=[pltpu.VMEM((hbb + tm, W_B), F32), pltpu.VMEM((hbc + tm, W_C), F32),
                        pltpu.VMEM((SUBLANES, tm + SUBLANES * ((CONV_B - 1) // SUBLANES), W_B), F32)],
        compiler_params=_cparams(("arbitrary",)),
        name="mix_ln1",
    )(x, ya, glu, uc, go, hist_b, hist_c, lw['convb_w'], lw['convb_b'], lw['lnb_g'], lw['lnb_b'],
      lw['convc_w'], lw['w_out'], lw['ln1_g'], lw['ln1_b'])


def _cross_kernel(x_ref, mk_ref, mv_ref, wq_ref, wo_ref, g_ref, b_ref, o_ref, qx_ref, acc_ref,
                  *, tm, n_groups, n_mem, interleaved, alpha):
    g = pl.program_id(1)

    @pl.when(g == 0)
    def _():
        qx_ref[...] = (_dot(x_ref[...].astype(BF16), wq_ref[...]) * (HD_X ** -0.5)).astype(BF16)
        acc_ref[...] = jnp.zeros_like(acc_ref)

    def head_mem(ref, h):
        if not interleaved:
            return ref[:, h * HD_X:(h + 1) * HD_X].astype(BF16)
        tiles = [ref[pl.ds(dt * H_X + h, n_mem, stride=MEM_ROWS), :] for dt in range(HD_X // LANES)]
        return jnp.concatenate(tiles, axis=1).astype(BF16)

    heads = []
    head_cols = [slice(h * HD_X, (h + 1) * HD_X) for h in range(H_X)]
    scores = [_dot_nt(qx_ref[:, head_cols[h]], head_mem(mk_ref, h)) for h in range(H_X)]
    for h, s in enumerate(scores):
        s = s - jnp.max(s, axis=-1, keepdims=True)
        e = jnp.exp(s)
        pr = e / jnp.sum(e, axis=-1, keepdims=True)
        heads.append(_dot(pr.astype(BF16), head_mem(mv_ref, h)))
    o = jnp.concatenate(heads, axis=-1)
    if n_groups == 1:
        acc_ref[...] = o
    else:
        rgrp = (pl.program_id(0) * tm + lax.broadcasted_iota(jnp.int32, (tm, 1), 0)) % n_groups
        acc_ref[...] = jnp.where(rgrp == g, o, acc_ref[...])

    @pl.when(g == n_groups - 1)
    def _():
        ox = _dot(acc_ref[...].astype(BF16), wo_ref[...])
        o_ref[...] = _layer_norm(alpha * x_ref[...] + ox, g_ref[...], b_ref[...])


def _interleave_mem(cache_mem):
    depth, b, n_mem = cache_mem.shape[:3]
    v = cache_mem.reshape(depth, b, n_mem, H_X, HD_X // LANES, LANES).transpose(0, 1, 2, 4, 3, 5)
    return v.reshape(depth, b, n_mem * MEM_ROWS, LANES)


def _cross(x, mem_k, mem_v, lw, *, alpha, layer=None):
    t = x.shape[0]
    tm = _row_tile(t)
    row = pl.BlockSpec((tm, D_MODEL), lambda i, g: (i, 0))
    interleaved = layer is not None
    if interleaved:
        n_groups, n_mem = mem_k.shape[1], mem_k.shape[2] // MEM_ROWS
        mem = pl.BlockSpec((None, None, n_mem * MEM_ROWS, LANES), lambda i, g: (layer, g, 0, 0))
    else:
        n_groups, n_mem, _ = mem_k.shape
        mem = pl.BlockSpec((None, n_mem, D_MODEL), lambda i, g: (g, 0, 0))
    cst = lambda r, n: pl.BlockSpec((r, n), lambda i, g: (0, 0))
    return pl.pallas_call(
        functools.partial(_cross_kernel, tm=tm, n_groups=n_groups, n_mem=n_mem, interleaved=interleaved,
                          alpha=alpha),
        grid=(t // tm, n_groups),
        in_specs=[row, mem, mem, cst(D_MODEL, D_MODEL), cst(D_MODEL, D_MODEL), cst(1, D_MODEL), cst(1, D_MODEL)],
        out_specs=row,
        out_shape=jax.ShapeDtypeStruct((t, D_MODEL), F32),
        scratch_shapes=[pltpu.VMEM((tm, D_MODEL), BF16), pltpu.VMEM((tm, D_MODEL), F32)],
        compiler_params=_cparams(("arbitrary", "arbitrary")),
        name="cross_ln2",
    )(x, mem_k, mem_v, lw['w_qx'], lw['w_ox'], lw['ln2_g'], lw['ln2_b'])


def _ffn_kernel(x_ref, hf_ref, wg_ref, wu_ref, cfw_ref, cfb_ref, wd_ref, g_ref, b_ref, o_ref, st_ref,
                carry, ext, acc_ref, *, tm, shift, hbf, alpha):
    @pl.when(pl.program_id(0) == 0)
    def _():
        carry[...] = hf_ref[...]

    xb = x_ref[...].astype(BF16)
    off = hbf - (CONV_F - 1) * shift
    n_chunks = D_FF // FF_CHUNK

    def gate_up(c):
        cols = slice(c * FF_CHUNK, (c + 1) * FF_CHUNK)
        return _dot(xb, wg_ref[:, cols]), _dot(xb, wu_ref[:, cols])

    nxt = gate_up(0)
    for c in range(n_chunks):
        cols = slice(c * FF_CHUNK, (c + 1) * FF_CHUNK)
        gate, up = nxt
        if c + 1 < n_chunks:
            nxt = gate_up(c + 1)
        ext[0:hbf, :] = carry[:, cols]
        ext[hbf:hbf + tm, :] = gate
        gf = jnp.broadcast_to(cfb_ref[:, cols], (tm, FF_CHUNK))
        for w in range(CONV_F):
            gf = gf + ext[off + w * shift:off + w * shift + tm, :] * cfw_ref[w:w + 1, cols]
        carry[:, cols] = ext[tm:tm + hbf, :]
        hid = gf * _sigmoid(gf) * up
        part = _dot(hid.astype(BF16), wd_ref[cols, :])
        if c == 0:
            acc_ref[...] = part
        else:
            acc_ref[...] += part
    o_ref[...] = _layer_norm(alpha * x_ref[...] + acc_ref[...], g_ref[...], b_ref[...])
    st_ref[...] = carry[...]


def _ffn(x, hist_f, lw, *, shift, alpha):
    t = x.shape[0]
    tm = _row_tile(t)
    hbf = hist_f.shape[0]
    row = pl.BlockSpec((tm, D_MODEL), lambda i: (i, 0))
    return pl.pallas_call(
        functools.partial(_ffn_kernel, tm=tm, shift=shift, hbf=hbf, alpha=alpha),
        grid=(t // tm,),
        in_specs=[row, _full((hbf, D_FF)), _full((D_MODEL, D_FF)), _full((D_MODEL, D_FF)),
                  _full((CONV_F, D_FF)), _full((1, D_FF)), _full((D_FF, D_MODEL)), _full((1, D_MODEL)),
                  _full((1, D_MODEL))],
        out_specs=[row, _full((hbf, D_FF))],
        out_shape=[jax.ShapeDtypeStruct((t, D_MODEL), F32), jax.ShapeDtypeStruct((hbf, D_FF), F32)],
        scratch_shapes=[pltpu.VMEM((hbf, D_FF), F32), pltpu.VMEM((hbf + tm, FF_CHUNK), F32),
                        pltpu.VMEM((tm, D_MODEL), F32)],
        compiler_params=_cparams(("arbitrary",)),
        name="ffn_ln3",
    )(x, hist_f, lw['w_gate'], lw['w_up'], lw['convf_w'], lw['convf_b'], lw['w_down'], lw['ln3_g'], lw['ln3_b'])


def _mem_proj_kernel(m_ref, wk_ref, wv_ref, k_ref, v_ref):
    mb = m_ref[...].astype(BF16)
    k_ref[...] = _dot(mb, wk_ref[...])
    v_ref[...] = _dot(mb, wv_ref[...])


def _mem_proj(mem, wk_bf, wv_bf):
    n = mem.shape[0]
    sd = jax.ShapeDtypeStruct((n, D_MODEL), F32)
    return pl.pallas_call(
        _mem_proj_kernel, out_shape=[sd, sd],
        compiler_params=pltpu.CompilerParams(vmem_limit_bytes=V7X_VMEM_LIMIT),
        name="mem_proj",
    )(mem, wk_bf, wv_bf)


def kernel(x_prompt, x_sample, mem_prompt, cache_k, cache_v, page_table, cache_mem_k, cache_mem_v, state_conv_b, state_conv_c, state_conv_f, rel_bias, w_in, w_out, convb_w, convb_b, lnb_g, lnb_b, convc_w, ln1_g, ln1_b, w_qx, w_kx, w_vx, w_ox, ln2_g, ln2_b, w_gate, w_up, convf_w, convf_b, w_down, ln3_g, ln3_b):
    bp, seq, _ = x_prompt.shape
    bs, ds, _ = x_sample.shape
    depth = w_in.shape[0]
    n_mem = mem_prompt.shape[1]
    alpha = _alpha(depth)
    assert bp == 1 and seq % MOBA_BLOCK == 0

    xp = x_prompt.reshape(seq, D_MODEL)
    xs = x_sample.transpose(1, 0, 2).reshape(ds * bs, D_MODEL)
    mem_p = mem_prompt.reshape(n_mem, D_MODEL)
    zeros_b = jnp.zeros((32, W_B), F32)
    zeros_c = jnp.zeros((8, W_C), F32)
    zeros_f = jnp.zeros((8, D_FF), F32)
    mem_k_sample = _interleave_mem(cache_mem_k)
    mem_v_sample = _interleave_mem(cache_mem_v)

    outs = {n: [] for n in ('kp', 'vp', 'ks', 'vs', 'mkp', 'mvp', 'cbp', 'cbs', 'ccp', 'ccs', 'cfp', 'cfs')}
    for l in range(depth):
        vec = lambda a: a[l].reshape(1, -1)
        lw = {'w_out': w_out[l].astype(BF16), 'convb_w': convb_w[l], 'convb_b': vec(convb_b),
              'lnb_g': vec(lnb_g), 'lnb_b': vec(lnb_b), 'convc_w': convc_w[l], 'ln1_g': vec(ln1_g),
              'ln1_b': vec(ln1_b), 'w_qx': w_qx[l].astype(BF16), 'w_ox': w_ox[l].astype(BF16),
              'ln2_g': vec(ln2_g), 'ln2_b': vec(ln2_b), 'w_gate': w_gate[l].astype(BF16),
              'w_up': w_up[l].astype(BF16), 'convf_w': convf_w[l], 'convf_b': vec(convf_b),
              'w_down': w_down[l].astype(BF16), 'ln3_g': vec(ln3_g), 'ln3_b': vec(ln3_b)}
        w_in_bf = w_in[l].astype(BF16)

        qt, kb, vt, k, v, glu, uc, go, kmean = _in_proj(xp, w_in_bf, prompt=True)
        q_s, k_s, v_s, glu_s, uc_s, go_s = _in_proj(xs, w_in_bf, prompt=False)
        seq_major = lambda a: a.reshape(ds, bs, -1).transpose(1, 0, 2)
        q_sm, k_sm, v_sm = seq_major(q_s), seq_major(k_s), seq_major(v_s)
        pages_per_step = _fused_pages_per_step((H_A // 2) * (seq // MOBA_BLOCK), bs, page_table.shape[1])
        if pages_per_step is None:
            ya = _moba_prompt(rel_bias, qt, kb, vt, kmean)
            ya_s = _moba_sample(l, rel_bias, page_table, cache_k, cache_v, q_sm, k_sm, v_sm)
        else:
            ya, ya_s = _moba_fused(l, rel_bias, qt, kb, vt, kmean, page_table, cache_k, cache_v,
                                   q_sm, k_sm, v_sm, pages_per_step)

        mk, mv = _mem_proj(mem_p, w_kx[l].astype(BF16), w_vx[l].astype(BF16))
        x1 = _mix(xp, ya, glu, uc, go, zeros_b, zeros_c, lw, shift=1, alpha=alpha)
        x2 = _cross(x1, mk[None], mv[None], lw, alpha=alpha)
        xp, st_f = _ffn(x2, zeros_f, lw, shift=1, alpha=alpha)
        from_rows = lambda a: a.reshape(1, H_A, HD_A, seq).transpose(0, 3, 1, 2)
        outs['kp'].append(from_rows(k))
        outs['vp'].append(from_rows(v))
        outs['mkp'].append(mk.reshape(1, n_mem, H_X, HD_X))
        outs['mvp'].append(mv.reshape(1, n_mem, H_X, HD_X))
        outs['cbp'].append(glu[None, seq - (CONV_B - 1):])
        outs['ccp'].append(uc[None, seq - (CONV_C - 1):])
        outs['cfp'].append(st_f[None, st_f.shape[0] - (CONV_F - 1):])

        hb = state_conv_b[l].transpose(1, 0, 2)
        hc = state_conv_c[l].transpose(1, 0, 2)
        hf = state_conv_f[l].transpose(1, 0, 2)
        glu, uc, go = glu_s, uc_s, go_s
        ya = ya_s.transpose(1, 0, 2).reshape(ds * bs, W_A)
        x1 = _mix(xs, ya, glu, uc, go, hb.reshape(-1, W_B), hc.reshape(-1, W_C), lw, shift=bs, alpha=alpha)
        x2 = _cross(x1, mem_k_sample, mem_v_sample, lw, alpha=alpha, layer=l)
        xs, st_f = _ffn(x2, hf.reshape(-1, D_FF), lw, shift=bs, alpha=alpha)
        outs['ks'].append(k_sm.reshape(bs, ds, H_A, HD_A))
        outs['vs'].append(v_sm.reshape(bs, ds, H_A, HD_A))
        tail = lambda hist, new, n: jnp.concatenate([hist, new.reshape(ds, bs, -1)], axis=0)[-n:].transpose(1, 0, 2)
        outs['cbs'].append(tail(hb, glu, CONV_B - 1))
        outs['ccs'].append(tail(hc, uc, CONV_C - 1))
        outs['cfs'].append(st_f.reshape(CONV_F - 1, bs, D_FF).transpose(1, 0, 2))

    y_prompt = xp.reshape(1, seq, D_MODEL)
    y_sample = xs.reshape(ds, bs, D_MODEL).transpose(1, 0, 2)
    st = lambda n, axis: jnp.stack(outs[n], axis=axis)
    return (y_prompt, y_sample, st('kp', 1), st('vp', 1), st('ks', 1), st('vs', 1), st('mkp', 0), st('mvp', 0),
            st('cbp', 0), st('cbs', 0), st('ccp', 0), st('ccs', 0), st('cfp', 0), st('cfs', 0))
```

```python
import functools
import math

import numpy as np
import jax
import jax.numpy as jnp
from jax import lax
from jax.experimental import pallas as pl
from jax.experimental.pallas import tpu as pltpu

F32 = jnp.float32
BF16 = jnp.bfloat16

D_MODEL = 1024
PAGE_SIZE = 128
HD_A = 64
W_A = 512
H_A = 8
W_B = 256
W_C = 256
PROJ_IN = 2816
MOBA_BLOCK = 256
MOBA_TOPK = 3
CONV_B = 31
CONV_C = 3
CONV_F = 3
D_FF = 2816
FF_CHUNK = 256
H_X = 4
HD_X = 256
REL_BUCKETS = 32
REL_MAX_DIST = 128
LN_EPS = 1e-5
SUBLANES = 8
LANES = 128
MEM_ROWS = H_X * (HD_X // LANES)

NEG = -1e30
VROWS = 80
STEP_BLOCKS = 2
FAR_UNROLL = 4
FAR_TAIL_UNROLL = 2
LOG2E = math.log2(math.e)
V7X_VMEM_LIMIT = 56 * 1024 * 1024
ROW_TILE = 512
SAMPLE_PAGES_PER_STEP = 32


def _alpha(depth):
    return (2 * depth) ** 0.25


def _row_tile(t):
    tm = min(ROW_TILE, t)
    assert t % tm == 0 and (tm % SUBLANES == 0 or tm == t)
    return tm


def _cparams(sem):
    return pltpu.CompilerParams(dimension_semantics=sem, vmem_limit_bytes=V7X_VMEM_LIMIT)


def _full(shape):
    return pl.BlockSpec(shape, lambda *_: (0,) * len(shape))


def _sigmoid(x):
    return 1.0 / (1.0 + jnp.exp(-x))


def _layer_norm(h, g, b):
    mu = jnp.mean(h, axis=-1, keepdims=True)
    d = h - mu
    var = jnp.mean(d * d, axis=-1, keepdims=True)
    return d * lax.rsqrt(var + LN_EPS) * g + b


def _dot(a, b):
    return jnp.dot(a, b, preferred_element_type=F32)


def _dot_nt(a, b):
    return lax.dot_general(a, b, (((1,), (1,)), ((), ())), preferred_element_type=F32)


def _rel_bucket_np(dist):
    n = np.maximum(dist, 0)
    max_exact = REL_BUCKETS // 2
    nf = np.maximum(n, max_exact).astype(np.float32)
    large = max_exact + (np.log(nf / np.float32(max_exact)) / np.float32(math.log(REL_MAX_DIST / max_exact))
                         * np.float32(REL_BUCKETS - max_exact)).astype(np.int32)
    return np.where(n < max_exact, n, np.minimum(large, REL_BUCKETS - 1)).astype(np.int32)


def _in_proj_kernel(x_ref, w_ref, *outs, prompt, tm):
    xb = x_ref[...].astype(BF16)

    def proj(lo, hi):
        return _dot(xb, w_ref[:, lo:hi])

    q = proj(0, W_A) * (HD_A ** -0.5 * (LOG2E if prompt else 1.0))
    k = proj(W_A, 2 * W_A)
    v = proj(2 * W_A, 3 * W_A)
    o = 3 * W_A
    glu = proj(o, o + W_B) * _sigmoid(proj(o + W_B, o + 2 * W_B))
    o += 2 * W_B
    gate_out = proj(o, o + W_C)
    uc = proj(o + W_C, o + 2 * W_C) * proj(o + 2 * W_C, o + 3 * W_C)
    if prompt:
        qt_ref, kb_ref, vt_ref, k_ref, v_ref, glu_ref, uc_ref, go_ref, km_ref = outs
        for c in range(W_A // LANES):
            cols = slice(c * LANES, (c + 1) * LANES)
            qt_ref[cols, :] = q[:, cols].T.astype(BF16)
            k_ref[cols, :] = k[:, cols].T
            v_ref[cols, :] = v[:, cols].T
        kb_ref[...] = k.astype(BF16)
        for s in range(tm // MOBA_BLOCK):
            rows = slice(s * MOBA_BLOCK, (s + 1) * MOBA_BLOCK)
            for c in range(W_A // LANES):
                vt2 = v[rows, c * LANES:(c + 1) * LANES].T.astype(BF16)
                for hh in range(2):
                    r0 = (2 * c + hh) * VROWS
                    vt_ref[s, r0:r0 + HD_A, :] = vt2[hh * HD_A:(hh + 1) * HD_A, :]
                    vt_ref[s, r0 + HD_A:r0 + VROWS, :] = jnp.ones((VROWS - HD_A, MOBA_BLOCK), BF16)
            km_ref[s] = jnp.sum(k[rows, :], axis=0, keepdims=True) * (1.0 / MOBA_BLOCK)
    else:
        q_ref, k_ref, v_ref, glu_ref, uc_ref, go_ref = outs
        q_ref[...] = q
        k_ref[...] = k
        v_ref[...] = v
    glu_ref[...] = glu
    uc_ref[...] = uc
    go_ref[...] = gate_out


def _in_proj(x, w_bf, *, prompt):
    t = x.shape[0]
    tm = _row_tile(t)
    nb = t // MOBA_BLOCK
    row = lambda n: pl.BlockSpec((tm, n), lambda i: (i, 0))
    sd = jax.ShapeDtypeStruct
    if prompt:
        assert tm % MOBA_BLOCK == 0
        spb = tm // MOBA_BLOCK
        out_shape = [sd((W_A, t), BF16), sd((t, W_A), BF16), sd((nb, H_A * VROWS, MOBA_BLOCK), BF16),
                     sd((W_A, t), F32), sd((W_A, t), F32), sd((t, W_B), F32), sd((t, W_C), F32),
                     sd((t, W_C), F32), sd((nb, 1, W_A), F32)]
        col = pl.BlockSpec((W_A, tm), lambda i: (0, i))
        out_specs = [col, row(W_A),
                     pl.BlockSpec((spb, H_A * VROWS, MOBA_BLOCK), lambda i: (i, 0, 0)),
                     col, col, row(W_B), row(W_C), row(W_C),
                     pl.BlockSpec((spb, 1, W_A), lambda i: (i, 0, 0))]
    else:
        out_shape = [sd((t, W_A), F32), sd((t, W_A), F32), sd((t, W_A), F32), sd((t, W_B), F32),
                     sd((t, W_C), F32), sd((t, W_C), F32)]
        out_specs = [row(W_A), row(W_A), row(W_A), row(W_B), row(W_C), row(W_C)]
    return pl.pallas_call(
        functools.partial(_in_proj_kernel, prompt=prompt, tm=tm),
        grid=(t // tm,),
        in_specs=[row(D_MODEL), _full((D_MODEL, PROJ_IN))],
        out_specs=out_specs, out_shape=out_shape,
        compiler_params=_cparams(("arbitrary",)),
        name="in_proj_prompt" if prompt else "in_proj_sample",
    )(x, w_bf)


def _moba_prompt_kernel(*refs, nb):
    _moba_prompt_step(pl.program_id(0), pl.program_id(1), *refs, nb=nb)


def _moba_prompt_step(p, i, rb_ref, bko_ref, bkp_ref, qt_ref, k_ref, vt_ref, km_ref, o_ref,
                      bown, bprev, sel_ref, m_ref, acc_ref, sping, spong, *, nb):
    blk = MOBA_BLOCK

    @pl.when(i == 0)
    def _():
        bko = bko_ref[...]
        bkp = bkp_ref[...]
        for hh in range(2):
            h = 2 * p + hh
            far = rb_ref[REL_BUCKETS - 1, h]
            bo = jnp.full((blk, blk), NEG, F32)
            bp = jnp.zeros((blk, blk), F32)
            for b in range(REL_BUCKETS):
                val = (rb_ref[b, h] - far) * LOG2E
                bo = jnp.where(bko == b, val, bo)
                bp = jnp.where(bkp == b, val, bp)
            bown[hh] = bo
            bprev[hh] = bp

    qt = qt_ref[...]
    rowi = lax.broadcasted_iota(jnp.int32, qt.shape, 0)
    zero = jnp.zeros_like(qt)
    qts = [jnp.where(rowi < HD_A, qt, zero), jnp.where(rowi >= HD_A, qt, zero)]

    km = km_ref[...].astype(BF16)
    bidx = lax.broadcasted_iota(jnp.int32, (nb, blk), 0)
    for hh in range(2):
        g = _dot(km, qts[hh])
        g = jnp.where(bidx < i, g, -jnp.inf)
        selected = jnp.zeros((nb, blk), jnp.bool_)
        for _ in range(MOBA_TOPK):
            mx = jnp.max(g, axis=0, keepdims=True)
            first = jnp.min(jnp.where(g == mx, bidx, nb), axis=0, keepdims=True)
            hit = (bidx == first) & (mx > -jnp.inf)
            selected = selected | hit
            g = jnp.where(hit, -jnp.inf, g)
        sel_ref[hh] = jnp.where(selected, 0.0, NEG)
        m_ref[hh] = jnp.full((1, blk), NEG, F32)
        acc_ref[hh] = jnp.zeros((VROWS, blk), F32)

    def scores(j, hh):
        return _dot(k_ref[j], qts[hh])

    def softmax_pv(tiles, hh):
        mx = None
        for s, j, colbias in tiles:
            bm = jnp.max(s, axis=0, keepdims=True)
            if colbias is not None:
                bm = bm + colbias
            mx = bm if mx is None else jnp.maximum(mx, bm)
        m_old = m_ref[hh]
        m_new = jnp.maximum(m_old, mx)
        a = jnp.exp2(m_old - m_new)
        pts, vts = [], []
        for s, j, colbias in tiles:
            c = m_new if colbias is None else m_new - colbias
            pts.append(jnp.exp2(s - c).astype(BF16))
            vts.append(vt_ref[j, hh * VROWS:(hh + 1) * VROWS, :])
        r = _dot(jnp.concatenate(vts, axis=1), jnp.concatenate(pts, axis=0))
        acc_ref[hh] = a * acc_ref[hh] + r
        m_ref[hh] = m_new

    n_far = jnp.maximum(i - 1, 0)
    jp = jnp.maximum(i - 1, 0)
    s_own = [scores(i, hh) + bown[hh] for hh in range(2)]
    s_prev = [scores(jp, hh) + bprev[hh] for hh in range(2)]
    for hh in range(2):
        for d in range(STEP_BLOCKS):
            sping[hh, d] = scores(jnp.minimum(d, jnp.maximum(n_far - 1, 0)), hh)
    for hh in range(2):
        prev_mask = jnp.where(i >= 1, sel_ref[hh, pl.ds(jp, 1), :], NEG)
        softmax_pv([(s_own[hh], i, None), (s_prev[hh], jp, prev_mask)], hh)

    bufs = [sping, spong]

    def far_loop(first, unroll, trips):
        last = n_far - 1

        def trip(t, carry):
            for u in range(unroll):
                cur, nxt = bufs[u % 2], bufs[1 - u % 2]
                j = first + (t * unroll + u) * STEP_BLOCKS
                for hh in range(2):
                    for d in range(STEP_BLOCKS):
                        nxt[hh, d] = scores(jnp.minimum(j + STEP_BLOCKS + d, last), hh)
                for hh in range(2):
                    tiles = []
                    for d in range(STEP_BLOCKS):
                        jc = jnp.minimum(j + d, last)
                        mask = jnp.where(j + d <= last, sel_ref[hh, pl.ds(jc, 1), :], NEG)
                        tiles.append((cur[hh, d], jc, mask))
                    softmax_pv(tiles, hh)
            return carry

        lax.fori_loop(0, trips, trip, 0)

    per_trip = STEP_BLOCKS * FAR_UNROLL
    per_tail_trip = STEP_BLOCKS * FAR_TAIL_UNROLL
    main = n_far // per_trip * per_trip
    far_loop(0, FAR_UNROLL, n_far // per_trip)
    far_loop(main, FAR_TAIL_UNROLL, (n_far - main + per_tail_trip - 1) // per_tail_trip)

    outs = []
    for hh in range(2):
        acc = acc_ref[hh]
        outs.append(acc[0:HD_A, :] / acc[HD_A:HD_A + 1, :])
    o_ref[...] = jnp.concatenate(outs, axis=0).T


def _moba_prompt(rel_bias, qt, kb, vt, kmean):
    t = qt.shape[1]
    nb = t // MOBA_BLOCK
    blk = MOBA_BLOCK
    c = np.arange(blk)[:, None]
    r = np.arange(blk)[None, :]
    bk_own = np.where(c <= r, _rel_bucket_np(r - c), -1).astype(np.int32)
    bk_prev = _rel_bucket_np(blk + r - c)
    smem = pl.BlockSpec(memory_space=pltpu.SMEM)
    return pl.pallas_call(
        functools.partial(_moba_prompt_kernel, nb=nb),
        grid=(H_A // 2, nb),
        in_specs=[smem, _full((blk, blk)), _full((blk, blk)),
                  pl.BlockSpec((LANES, blk), lambda p, i: (p, i)),
                  pl.BlockSpec((nb, blk, LANES), lambda p, i: (0, 0, p)),
                  pl.BlockSpec((nb, 2 * VROWS, blk), lambda p, i: (0, p, 0)),
                  pl.BlockSpec((nb, LANES), lambda p, i: (0, p))],
        out_specs=pl.BlockSpec((blk, LANES), lambda p, i: (i, p)),
        out_shape=jax.ShapeDtypeStruct((t, W_A), F32),
        scratch_shapes=[pltpu.VMEM((2, blk, blk), F32), pltpu.VMEM((2, blk, blk), F32),
                        pltpu.VMEM((2, nb, blk), F32), pltpu.VMEM((2, 1, blk), F32),
                        pltpu.VMEM((2, VROWS, blk), F32),
                        pltpu.VMEM((2, STEP_BLOCKS, blk, blk), F32), pltpu.VMEM((2, STEP_BLOCKS, blk, blk), F32)],
        compiler_params=_cparams(("arbitrary", "arbitrary")),
        name="moba_prompt",
    )(rel_bias, jnp.asarray(bk_own), jnp.asarray(bk_prev), qt, kb.reshape(nb, blk, W_A), vt,
      kmean.reshape(nb, W_A))


def _head_rows(x, n_tok):
    lane_head = lax.broadcasted_iota(jnp.int32, (H_A, W_A), 1) // HD_A
    sub = lax.broadcasted_iota(jnp.int32, (H_A, W_A), 0)
    parts = [jnp.where(lane_head == sub, jnp.broadcast_to(x[t:t + 1, :], (H_A, W_A)), 0.0)
             for t in range(n_tok)]
    return jnp.concatenate(parts, axis=0)


def _moba_sample_k_kernel(pt_ref, q_ref, kn_ref, rbr_ref, bkl_ref, *rest, u, nchunk, ds):
    kpages = rest[:u]
    p_ref, st_ref = rest[u], rest[u + 1]
    qbd_ref, s_ref, ks_ref = rest[u + 2:]
    _sample_keys_step(pl.program_id(1), q_ref, kn_ref, rbr_ref, bkl_ref, kpages, [p_ref], [st_ref],
                      qbd_ref, s_ref, ks_ref, u=u, nchunk=nchunk, ds=ds)


def _sample_keys_step(c, q_ref, kn_ref, rbr_ref, bkl_ref, kpages, p_refs, st_refs, qbd_ref, s_ref, ks_ref,
                      *, u, nchunk, ds):
    nrow = ds * H_A
    bpc = u // 2
    nbp = nchunk * bpc

    @pl.when(c == 0)
    def _():
        qbd_ref[...] = _head_rows(q_ref[...], ds).astype(BF16)
        ks_ref[...] = jnp.zeros_like(ks_ref)

    qbd = qbd_ref[...]
    ks = ks_ref[...]
    lane_blk = lax.broadcasted_iota(jnp.int32, ks.shape, 1)
    for v in range(bpc):
        k0, k1 = kpages[2 * v][...], kpages[2 * v + 1][...]
        kk = jnp.concatenate([k0.astype(BF16), k1.astype(BF16)], axis=1)
        s_ref[c, :, v * MOBA_BLOCK:(v + 1) * MOBA_BLOCK] = _dot(qbd, kk)
        col = jnp.sum(k0 + k1, axis=1, keepdims=True)
        ks = jnp.where(lane_blk == c * bpc + v, col, ks)
    ks_ref[...] = ks

    @pl.when(c == nchunk - 1)
    def _():
        blk = MOBA_BLOCK
        km = (ks_ref[...] * (1.0 / blk)).astype(BF16)
        g = _dot(qbd, km)
        lane = lax.broadcasted_iota(jnp.int32, g.shape, 1)
        g = jnp.where(lane < nbp, g, -jnp.inf)
        selected = jnp.zeros(g.shape, jnp.bool_)
        for _ in range(min(MOBA_TOPK, nbp)):
            mx = jnp.max(g, axis=1, keepdims=True)
            first = jnp.min(jnp.where(g == mx, lane, nbp), axis=1, keepdims=True)
            hit = (lane == first) & (mx > -jnp.inf)
            selected = selected | hit
            g = jnp.where(hit, -jnp.inf, g)
        maskadd = jnp.where(selected, 0.0, NEG)

        rbr = rbr_ref[...]
        far = rbr[:, REL_BUCKETS - 1:REL_BUCKETS]
        bkl = bkl_ref[...]
        bias_last = jnp.zeros((nrow, blk), F32)
        for b in range(REL_BUCKETS):
            bias_last = jnp.where(bkl == b, rbr[:, b:b + 1], bias_last)

        trow = lax.broadcasted_iota(jnp.int32, (nrow, 1), 0) // H_A
        qf = qbd.astype(F32)
        knb = kn_ref[...].astype(BF16).astype(F32)
        own = []
        for cc in range(ds):
            sc = jnp.sum(qf * knb[cc:cc + 1, :], axis=1, keepdims=True)
            bias = jnp.zeros((nrow, 1), F32)
            for d in range(ds - cc):
                bias = jnp.where(trow - cc == d, rbr[:, d:d + 1], bias)
            own.append(jnp.where(trow >= cc, sc + bias, NEG))

        def logits(n):
            cn, off = n // bpc, (n % bpc) * blk
            b = bias_last if n == nbp - 1 else far
            return s_ref[cn, :, off:off + blk] + b + maskadd[:, n:n + 1]

        mt = logits(0)
        for n in range(1, nbp):
            mt = jnp.maximum(mt, logits(n))
        m = jnp.max(mt, axis=1, keepdims=True)
        for cc in range(ds):
            m = jnp.maximum(m, own[cc])
        lt = jnp.zeros((nrow, blk), F32)
        for n in range(nbp):
            cn, off = n // bpc, (n % bpc) * blk
            pn = jnp.exp(logits(n) - m)
            lt = lt + pn
            for p_ref in p_refs:
                p_ref[cn, :, off:off + blk] = pn.astype(BF16)
        l = jnp.sum(lt, axis=1, keepdims=True)
        lane128 = lax.broadcasted_iota(jnp.int32, (nrow, LANES), 1)
        stats = jnp.zeros((nrow, LANES), F32)
        for cc in range(ds):
            po = jnp.exp(own[cc] - m)
            l = l + po
            stats = jnp.where(lane128 == cc, po, stats)
        for st_ref in st_refs:
            st_ref[...] = jnp.where(lane128 == ds, l, stats)


def _moba_sample_v_kernel(pt_ref, p_ref, st_ref, vn_ref, *rest, u, nchunk, ds):
    vpages = rest[:u]
    o_ref, acc_ref = rest[u], rest[u + 1]
    _sample_values_step(pl.program_id(1), p_ref, st_ref, vn_ref, vpages, o_ref, acc_ref, u=u, nchunk=nchunk, ds=ds)


def _sample_values_step(c, p_ref, st_ref, vn_ref, vpages, o_ref, acc_ref, *, u, nchunk, ds):

    @pl.when(c == 0)
    def _():
        acc_ref[...] = jnp.zeros_like(acc_ref)

    acc = acc_ref[...]
    for v in range(u // 2):
        vv = jnp.concatenate([vpages[2 * v][...].astype(BF16), vpages[2 * v + 1][...].astype(BF16)], axis=1)
        acc = acc + _dot_nt(p_ref[:, v * MOBA_BLOCK:(v + 1) * MOBA_BLOCK], vv)
    acc_ref[...] = acc

    @pl.when(c == nchunk - 1)
    def _():
        a = acc_ref[...]
        st = st_ref[...]
        vnb = vn_ref[...].astype(BF16).astype(F32)
        for cc in range(ds):
            a = a + st[:, cc:cc + 1].astype(BF16).astype(F32) * vnb[cc:cc + 1, :]
        a = a / st[:, ds:ds + 1]
        lane_head = lax.broadcasted_iota(jnp.int32, (H_A, W_A), 1) // HD_A
        sub = lax.broadcasted_iota(jnp.int32, (H_A, W_A), 0)
        for t in range(ds):
            rows = a[t * H_A:(t + 1) * H_A, :]
            o_ref[t:t + 1, :] = jnp.sum(jnp.where(lane_head == sub, rows, 0.0), axis=0, keepdims=True)


def _page_view(cache):
    n_pool, depth = cache.shape[:2]
    return cache.transpose(0, 1, 3, 4, 2).reshape(n_pool, depth, W_A, PAGE_SIZE)


def _sample_tables(rel_bias, ds):
    nrow = ds * H_A
    rbr = jnp.tile(rel_bias.T, (ds, 1))
    trow = np.arange(nrow)[:, None] // H_A
    bk_last = _rel_bucket_np(MOBA_BLOCK + trow - np.arange(MOBA_BLOCK)[None, :])
    return rbr, jnp.asarray(bk_last)


def _check_sample_shapes(n_pages, ds, u):
    past_len = n_pages * PAGE_SIZE
    assert past_len % MOBA_BLOCK == 0 and ds <= REL_BUCKETS // 2
    assert n_pages % u == 0 and u % 2 == 0 and past_len // MOBA_BLOCK <= LANES


def _moba_sample_values(layer, page_table, cv, probs, stats, v_new, u):
    b, ds, _ = v_new.shape
    nchunk = page_table.shape[1] // u
    nrow = ds * H_A
    page_spec = lambda j: pl.BlockSpec((None, None, W_A, PAGE_SIZE),
                                       lambda bi, c, pt: (pt[bi, c * u + j], layer, 0, 0))
    per_seq = lambda r, n: pl.BlockSpec((None, r, n), lambda bi, c, pt: (bi, 0, 0))
    return pl.pallas_call(
        functools.partial(_moba_sample_v_kernel, u=u, nchunk=nchunk, ds=ds),
        grid_spec=pltpu.PrefetchScalarGridSpec(
            num_scalar_prefetch=1, grid=(b, nchunk),
            in_specs=[pl.BlockSpec((None, None, nrow, u * PAGE_SIZE), lambda bi, c, pt: (bi, c, 0, 0)),
                      per_seq(nrow, LANES), per_seq(ds, W_A)] + [page_spec(j) for j in range(u)],
            out_specs=per_seq(ds, W_A),
            scratch_shapes=[pltpu.VMEM((nrow, W_A), F32)]),
        out_shape=jax.ShapeDtypeStruct((b, ds, W_A), F32),
        compiler_params=_cparams(("arbitrary", "arbitrary")),
        name="moba_sample_values",
    )(page_table, probs, stats, v_new, *([cv] * u))


def _moba_sample(layer, rel_bias, page_table, cache_k, cache_v, q, k_new, v_new):
    b, ds, _ = q.shape
    n_pages = page_table.shape[1]
    u = min(SAMPLE_PAGES_PER_STEP, n_pages)
    _check_sample_shapes(n_pages, ds, u)
    nchunk = n_pages // u
    nrow = ds * H_A
    ck, cv = _page_view(cache_k), _page_view(cache_v)
    rbr, bk_last = _sample_tables(rel_bias, ds)

    def page_spec(j):
        return pl.BlockSpec((None, None, W_A, PAGE_SIZE),
                            lambda bi, c, pt: (pt[bi, c * u + j], layer, 0, 0))

    per_seq = lambda r, n: pl.BlockSpec((None, r, n), lambda bi, c, pt: (bi, 0, 0))
    const = lambda r, n: pl.BlockSpec((r, n), lambda bi, c, pt: (0, 0))
    probs, stats = pl.pallas_call(
        functools.partial(_moba_sample_k_kernel, u=u, nchunk=nchunk, ds=ds),
        grid_spec=pltpu.PrefetchScalarGridSpec(
            num_scalar_prefetch=1, grid=(b, nchunk),
            in_specs=[per_seq(ds, W_A), per_seq(ds, W_A), const(nrow, REL_BUCKETS), const(nrow, MOBA_BLOCK)]
                     + [page_spec(j) for j in range(u)],
            out_specs=[pl.BlockSpec((None, nchunk, nrow, u * PAGE_SIZE), lambda bi, c, pt: (bi, 0, 0, 0)),
                       per_seq(nrow, LANES)],
            scratch_shapes=[pltpu.VMEM((nrow, W_A), BF16), pltpu.VMEM((nchunk, nrow, u * PAGE_SIZE), F32),
                            pltpu.VMEM((W_A, LANES), F32)]),
        out_shape=[jax.ShapeDtypeStruct((b, nchunk, nrow, u * PAGE_SIZE), BF16),
                   jax.ShapeDtypeStruct((b, nrow, LANES), F32)],
        compiler_params=_cparams(("arbitrary", "arbitrary")),
        name="moba_sample_keys",
    )(page_table, q, k_new, rbr, bk_last, *([ck] * u))
    return _moba_sample_values(layer, page_table, cv, probs, stats, v_new, u)


def _moba_fused_kernel(pt_ref, rb_ref, bko_ref, bkp_ref, qt_ref, k_ref, vt_ref, km_ref,
                       q_ref, kn_ref, vn_ref, rbr_ref, bkl_ref, *rest, nb, u, nchunk, ds):
    kpages, vpages = rest[:u], rest[u:2 * u]
    o_ref, p_out, st_out, y_ref = rest[2 * u:2 * u + 4]
    (bown, bprev, sel_ref, m_ref, acc_ref, sping, spong,
     qbd_ref, s_ref, ks_ref, p_scr, st_scr, accv_ref) = rest[2 * u + 4:]
    p = pl.program_id(0)
    i = pl.program_id(1)
    step = p * nb + i
    b = step // nchunk
    c = step % nchunk
    par = b % 2
    _sample_keys_step(c, q_ref, kn_ref, rbr_ref, bkl_ref, kpages, [p_out, p_scr.at[par]],
                      [st_out, st_scr.at[par]], qbd_ref, s_ref, ks_ref, u=u, nchunk=nchunk, ds=ds)

    @pl.when(b >= 1)
    def _():
        _sample_values_step(c, p_scr.at[1 - par, c], st_scr.at[1 - par], vn_ref, vpages, y_ref, accv_ref,
                            u=u, nchunk=nchunk, ds=ds)

    _moba_prompt_step(p, i, rb_ref, bko_ref, bkp_ref, qt_ref, k_ref, vt_ref, km_ref, o_ref,
                      bown, bprev, sel_ref, m_ref, acc_ref, sping, spong, nb=nb)


def _fused_pages_per_step(n_steps, bs, n_pages):
    if bs < 2 or (bs * n_pages) % n_steps:
        return None
    u = bs * n_pages // n_steps
    if u < 2 or u % 2 or u > SAMPLE_PAGES_PER_STEP or n_pages % u:
        return None
    return u


def _moba_fused(layer, rel_bias, qt, kb, vt, kmean, page_table, cache_k, cache_v, q, k_new, v_new, u):
    t = qt.shape[1]
    nb = t // MOBA_BLOCK
    blk = MOBA_BLOCK
    bs, ds, _ = q.shape
    n_pages = page_table.shape[1]
    _check_sample_shapes(n_pages, ds, u)
    nchunk = n_pages // u
    assert bs * nchunk == (H_A // 2) * nb
    nrow = ds * H_A
    ck, cv = _page_view(cache_k), _page_view(cache_v)
    rbr, bk_last = _sample_tables(rel_bias, ds)
    c_ = np.arange(blk)[:, None]
    r_ = np.arange(blk)[None, :]
    bk_own = np.where(c_ <= r_, _rel_bucket_np(r_ - c_), -1).astype(np.int32)
    bk_prev = _rel_bucket_np(blk + r_ - c_)

    seq_k = lambda p, i: (p * nb + i) // nchunk
    seq_v = lambda p, i: jnp.maximum(seq_k(p, i) - 1, 0)
    chunk = lambda p, i: (p * nb + i) % nchunk
    const = lambda shape: pl.BlockSpec(shape, lambda p, i, pt: (0,) * len(shape))
    per_k = lambda r, n: pl.BlockSpec((None, r, n), lambda p, i, pt: (seq_k(p, i), 0, 0))
    per_v = lambda r, n: pl.BlockSpec((None, r, n), lambda p, i, pt: (seq_v(p, i), 0, 0))
    kpage = lambda j: pl.BlockSpec((None, None, W_A, PAGE_SIZE),
                                   lambda p, i, pt: (pt[seq_k(p, i), chunk(p, i) * u + j], layer, 0, 0))
    vpage = lambda j: pl.BlockSpec((None, None, W_A, PAGE_SIZE),
                                   lambda p, i, pt: (pt[seq_v(p, i), chunk(p, i) * u + j], layer, 0, 0))
    smem = pl.BlockSpec(memory_space=pltpu.SMEM)
    ya, probs, stats, ys = pl.pallas_call(
        functools.partial(_moba_fused_kernel, nb=nb, u=u, nchunk=nchunk, ds=ds),
        grid_spec=pltpu.PrefetchScalarGridSpec(
            num_scalar_prefetch=1, grid=(H_A // 2, nb),
            in_specs=[smem, const((blk, blk)), const((blk, blk)),
                      pl.BlockSpec((LANES, blk), lambda p, i, pt: (p, i)),
                      pl.BlockSpec((nb, blk, LANES), lambda p, i, pt: (0, 0, p)),
                      pl.BlockSpec((nb, 2 * VROWS, blk), lambda p, i, pt: (0, p, 0)),
                      pl.BlockSpec((nb, LANES), lambda p, i, pt: (0, p)),
                      per_k(ds, W_A), per_k(ds, W_A), per_v(ds, W_A), const((nrow, REL_BUCKETS)),
                      const((nrow, blk))]
                     + [kpage(j) for j in range(u)] + [vpage(j) for j in range(u)],
            out_specs=[pl.BlockSpec((blk, LANES), lambda p, i, pt: (i, p)),
                       pl.BlockSpec((None, nchunk, nrow, u * PAGE_SIZE), lambda p, i, pt: (seq_k(p, i), 0, 0, 0)),
                       per_k(nrow, LANES), per_v(ds, W_A)],
            scratch_shapes=[pltpu.VMEM((2, blk, blk), F32), pltpu.VMEM((2, blk, blk), F32),
                            pltpu.VMEM((2, nb, blk), F32), pltpu.VMEM((2, 1, blk), F32),
                            pltpu.VMEM((2, VROWS, blk), F32),
                            pltpu.VMEM((2, STEP_BLOCKS, blk, blk), F32), pltpu.VMEM((2, STEP_BLOCKS, blk, blk), F32),
                            pltpu.VMEM((nrow, W_A), BF16), pltpu.VMEM((nchunk, nrow, u * PAGE_SIZE), F32),
                            pltpu.VMEM((W_A, LANES), F32),
                            pltpu.VMEM((2, nchunk, nrow, u * PAGE_SIZE), BF16), pltpu.VMEM((2, nrow, LANES), F32),
                            pltpu.VMEM((nrow, W_A), F32)]),
        out_shape=[jax.ShapeDtypeStruct((t, W_A), F32),
                   jax.ShapeDtypeStruct((bs, nchunk, nrow, u * PAGE_SIZE), BF16),
                   jax.ShapeDtypeStruct((bs, nrow, LANES), F32),
                   jax.ShapeDtypeStruct((bs - 1, ds, W_A), F32)],
        compiler_params=_cparams(("arbitrary", "arbitrary")),
        name="moba_fused",
    )(page_table, rel_bias, jnp.asarray(bk_own), jnp.asarray(bk_prev), qt, kb.reshape(nb, blk, W_A), vt,
      kmean.reshape(nb, W_A), q, k_new, v_new, rbr, bk_last, *([ck] * u), *([cv] * u))
    y_last = _moba_sample_values(layer, page_table[bs - 1:], cv, probs[bs - 1:], stats[bs - 1:], v_new[bs - 1:], u)
    return ya, jnp.concatenate([ys, y_last], axis=0)


def _mix_kernel(x_ref, ya_ref, glu_ref, uc_ref, go_ref, hb_ref, hc_ref, cbw_ref, cbb_ref, lbg_ref, lbb_ref,
                ccw_ref, wo_ref, g1_ref, b1_ref, o_ref, extb, extc, shb, *, tm, shift, hbb, hbc, alpha):
    @pl.when(pl.program_id(0) == 0)
    def _():
        extb[0:hbb, :] = hb_ref[...]
        extc[0:hbc, :] = hc_ref[...]

    mix_a = _dot(ya_ref[...].astype(BF16), wo_ref[0:W_A, :])
    extb[hbb:hbb + tm, :] = glu_ref[...]
    extc[hbc:hbc + tm, :] = uc_ref[...]
    offb = hbb - (CONV_B - 1) * shift
    if shift != 1:
        tap = lambda w: extb[offb + w * shift:offb + w * shift + tm, :]
    else:
        for r in range(SUBLANES):
            rows = tm + SUBLANES * ((CONV_B - 1 - r) // SUBLANES)
            shb[r, 0:rows, :] = extb[offb + r:offb + r + rows, :]
        tap = lambda w: shb[w % SUBLANES, w - w % SUBLANES:w - w % SUBLANES + tm, :]
    cb = jnp.broadcast_to(cbb_ref[...], (tm, W_B))
    for w in range(CONV_B):
        cb = cb + tap(w) * cbw_ref[w:w + 1, :]
    yb = _layer_norm(cb, lbg_ref[...], lbb_ref[...])
    yb = yb * _sigmoid(yb)
    offc = hbc - (CONV_C - 1) * shift
    cc = extc[offc:offc + tm, :] * ccw_ref[0:1, :]
    for w in range(1, CONV_C):
        cc = cc + extc[offc + w * shift:offc + w * shift + tm, :] * ccw_ref[w:w + 1, :]
    yc = go_ref[...] * cc
    mix = (mix_a
           + _dot(yb.astype(BF16), wo_ref[W_A:W_A + W_B, :])
           + _dot(yc.astype(BF16), wo_ref[W_A + W_B:W_A + W_B + W_C, :]))
    o_ref[...] = _layer_norm(alpha * x_ref[...] + mix, g1_ref[...], b1_ref[...])
    nb_rows = extb[tm:tm + hbb, :]
    nc_rows = extc[tm:tm + hbc, :]
    extb[0:hbb, :] = nb_rows
    extc[0:hbc, :] = nc_rows


def _mix(x, ya, glu, uc, go, hist_b, hist_c, lw, *, shift, alpha):
    t = x.shape[0]
    tm = _row_tile(t)
    hbb, hbc = hist_b.shape[0], hist_c.shape[0]
    row = lambda n: pl.BlockSpec((tm, n), lambda i: (i, 0))
    return pl.pallas_call(
        functools.partial(_mix_kernel, tm=tm, shift=shift, hbb=hbb, hbc=hbc, alpha=alpha),
        grid=(t // tm,),
        in_specs=[row(D_MODEL), row(W_A), row(W_B), row(W_C), row(W_C), _full((hbb, W_B)), _full((hbc, W_C)),
                  _full((CONV_B, W_B)), _full((1, W_B)), _full((1, W_B)), _full((1, W_B)), _full((CONV_C, W_C)),
                  _full((D_MODEL, D_MODEL)), _full((1, D_MODEL)), _full((1, D_MODEL))],
        out_specs=row(D_MODEL),
        out_shape=jax.ShapeDtypeStruct((t, D_MODEL), F32),
        scratch_shapes=[pltpu.VMEM((hbb + tm, W_B), F32), pltpu.VMEM((hbc + tm, W_C), F32),
                        pltpu.VMEM((SUBLANES, tm + SUBLANES * ((CONV_B - 1) // SUBLANES), W_B), F32)],
        compiler_params=_cparams(("arbitrary",)),
        name="mix_ln1",
    )(x, ya, glu, uc, go, hist_b, hist_c, lw['convb_w'], lw['convb_b'], lw['lnb_g'], lw['lnb_b'],
      lw['convc_w'], lw['w_out'], lw['ln1_g'], lw['ln1_b'])


def _cross_kernel(x_ref, mk_ref, mv_ref, wq_ref, wo_ref, g_ref, b_ref, o_ref, qx_ref, acc_ref,
                  *, tm, n_groups, n_mem, interleaved, alpha):
    g = pl.program_id(1)

    @pl.when(g == 0)
    def _():
        qx_ref[...] = (_dot(x_ref[...].astype(BF16), wq_ref[...]) * (HD_X ** -0.5)).astype(BF16)
        acc_ref[...] = jnp.zeros_like(acc_ref)

    def head_mem(ref, h):
        if not interleaved:
            return ref[:, h * HD_X:(h + 1) * HD_X].astype(BF16)
        tiles = [ref[pl.ds(dt * H_X + h, n_mem, stride=MEM_ROWS), :] for dt in range(HD_X // LANES)]
        return jnp.concatenate(tiles, axis=1).astype(BF16)

    heads = []
    head_cols = [slice(h * HD_X, (h + 1) * HD_X) for h in range(H_X)]
    scores = [_dot_nt(qx_ref[:, head_cols[h]], head_mem(mk_ref, h)) for h in range(H_X)]
    for h, s in enumerate(scores):
        s = s - jnp.max(s, axis=-1, keepdims=True)
        e = jnp.exp(s)
        pr = e / jnp.sum(e, axis=-1, keepdims=True)
        heads.append(_dot(pr.astype(BF16), head_mem(mv_ref, h)))
    o = jnp.concatenate(heads, axis=-1)
    if n_groups == 1:
        acc_ref[...] = o
    else:
        rgrp = (pl.program_id(0) * tm + lax.broadcasted_iota(jnp.int32, (tm, 1), 0)) % n_groups
        acc_ref[...] = jnp.where(rgrp == g, o, acc_ref[...])

    @pl.when(g == n_groups - 1)
    def _():
        ox = _dot(acc_ref[...].astype(BF16), wo_ref[...])
        o_ref[...] = _layer_norm(alpha * x_ref[...] + ox, g_ref[...], b_ref[...])


def _interleave_mem(cache_mem):
    depth, b, n_mem = cache_mem.shape[:3]
    v = cache_mem.reshape(depth, b, n_mem, H_X, HD_X // LANES, LANES).transpose(0, 1, 2, 4, 3, 5)
    return v.reshape(depth, b, n_mem * MEM_ROWS, LANES)


def _cross(x, mem_k, mem_v, lw, *, alpha, layer=None):
    t = x.shape[0]
    tm = _row_tile(t)
    row = pl.BlockSpec((tm, D_MODEL), lambda i, g: (i, 0))
    interleaved = layer is not None
    if interleaved:
        n_groups, n_mem = mem_k.shape[1], mem_k.shape[2] // MEM_ROWS
        mem = pl.BlockSpec((None, None, n_mem * MEM_ROWS, LANES), lambda i, g: (layer, g, 0, 0))
    else:
        n_groups, n_mem, _ = mem_k.shape
        mem = pl.BlockSpec((None, n_mem, D_MODEL), lambda i, g: (g, 0, 0))
    cst = lambda r, n: pl.BlockSpec((r, n), lambda i, g: (0, 0))
    return pl.pallas_call(
        functools.partial(_cross_kernel, tm=tm, n_groups=n_groups, n_mem=n_mem, interleaved=interleaved,
                          alpha=alpha),
        grid=(t // tm, n_groups),
        in_specs=[row, mem, mem, cst(D_MODEL, D_MODEL), cst(D_MODEL, D_MODEL), cst(1, D_MODEL), cst(1, D_MODEL)],
        out_specs=row,
        out_shape=jax.ShapeDtypeStruct((t, D_MODEL), F32),
        scratch_shapes=[pltpu.VMEM((tm, D_MODEL), BF16), pltpu.VMEM((tm, D_MODEL), F32)],
        compiler_params=_cparams(("arbitrary", "arbitrary")),
        name="cross_ln2",
    )(x, mem_k, mem_v, lw['w_qx'], lw['w_ox'], lw['ln2_g'], lw['ln2_b'])


def _ffn_kernel(x_ref, hf_ref, wg_ref, wu_ref, cfw_ref, cfb_ref, wd_ref, g_ref, b_ref, o_ref, st_ref,
                carry, ext, acc_ref, *, tm, shift, hbf, alpha):
    @pl.when(pl.program_id(0) == 0)
    def _():
        carry[...] = hf_ref[...]

    xb = x_ref[...].astype(BF16)
    off = hbf - (CONV_F - 1) * shift
    n_chunks = D_FF // FF_CHUNK

    def gate_up(c):
        cols = slice(c * FF_CHUNK, (c + 1) * FF_CHUNK)
        return _dot(xb, wg_ref[:, cols]), _dot(xb, wu_ref[:, cols])

    nxt = gate_up(0)
    for c in range(n_chunks):
        cols = slice(c * FF_CHUNK, (c + 1) * FF_CHUNK)
        gate, up = nxt
        if c + 1 < n_chunks:
            nxt = gate_up(c + 1)
        ext[0:hbf, :] = carry[:, cols]
        ext[hbf:hbf + tm, :] = gate
        gf = jnp.broadcast_to(cfb_ref[:, cols], (tm, FF_CHUNK))
        for w in range(CONV_F):
            gf = gf + ext[off + w * shift:off + w * shift + tm, :] * cfw_ref[w:w + 1, cols]
        carry[:, cols] = ext[tm:tm + hbf, :]
        hid = gf * _sigmoid(gf) * up
        part = _dot(hid.astype(BF16), wd_ref[cols, :])
        if c == 0:
            acc_ref[...] = part
        else:
            acc_ref[...] += part
    o_ref[...] = _layer_norm(alpha * x_ref[...] + acc_ref[...], g_ref[...], b_ref[...])
    st_ref[...] = carry[...]


def _ffn(x, hist_f, lw, *, shift, alpha):
    t = x.shape[0]
    tm = _row_tile(t)
    hbf = hist_f.shape[0]
    row = pl.BlockSpec((tm, D_MODEL), lambda i: (i, 0))
    return pl.pallas_call(
        functools.partial(_ffn_kernel, tm=tm, shift=shift, hbf=hbf, alpha=alpha),
        grid=(t // tm,),
        in_specs=[row, _full((hbf, D_FF)), _full((D_MODEL, D_FF)), _full((D_MODEL, D_FF)),
                  _full((CONV_F, D_FF)), _full((1, D_FF)), _full((D_FF, D_MODEL)), _full((1, D_MODEL)),
                  _full((1, D_MODEL))],
        out_specs=[row, _full((hbf, D_FF))],
        out_shape=[jax.ShapeDtypeStruct((t, D_MODEL), F32), jax.ShapeDtypeStruct((hbf, D_FF), F32)],
        scratch_shapes=[pltpu.VMEM((hbf, D_FF), F32), pltpu.VMEM((hbf + tm, FF_CHUNK), F32),
                        pltpu.VMEM((tm, D_MODEL), F32)],
        compiler_params=_cparams(("arbitrary",)),
        name="ffn_ln3",
    )(x, hist_f, lw['w_gate'], lw['w_up'], lw['convf_w'], lw['convf_b'], lw['w_down'], lw['ln3_g'], lw['ln3_b'])


def _mem_proj_kernel(m_ref, wk_ref, wv_ref, k_ref, v_ref):
    mb = m_ref[...].astype(BF16)
    k_ref[...] = _dot(mb, wk_ref[...])
    v_ref[...] = _dot(mb, wv_ref[...])


def _mem_proj(mem, wk_bf, wv_bf):
    n = mem.shape[0]
    sd = jax.ShapeDtypeStruct((n, D_MODEL), F32)
    return pl.pallas_call(
        _mem_proj_kernel, out_shape=[sd, sd],
        compiler_params=pltpu.CompilerParams(vmem_limit_bytes=V7X_VMEM_LIMIT),
        name="mem_proj",
    )(mem, wk_bf, wv_bf)


def kernel(x_prompt, x_sample, mem_prompt, cache_k, cache_v, page_table, cache_mem_k, cache_mem_v, state_conv_b, state_conv_c, state_conv_f, rel_bias, w_in, w_out, convb_w, convb_b, lnb_g, lnb_b, convc_w, ln1_g, ln1_b, w_qx, w_kx, w_vx, w_ox, ln2_g, ln2_b, w_gate, w_up, convf_w, convf_b, w_down, ln3_g, ln3_b):
    bp, seq, _ = x_prompt.shape
    bs, ds, _ = x_sample.shape
    depth = w_in.shape[0]
    n_mem = mem_prompt.shape[1]
    alpha = _alpha(depth)
    assert bp == 1 and seq % MOBA_BLOCK == 0

    xp = x_prompt.reshape(seq, D_MODEL)
    xs = x_sample.transpose(1, 0, 2).reshape(ds * bs, D_MODEL)
    mem_p = mem_prompt.reshape(n_mem, D_MODEL)
    zeros_b = jnp.zeros((32, W_B), F32)
    zeros_c = jnp.zeros((8, W_C), F32)
    zeros_f = jnp.zeros((8, D_FF), F32)
    mem_k_sample = _interleave_mem(cache_mem_k)
    mem_v_sample = _interleave_mem(cache_mem_v)

    outs = {n: [] for n in ('kp', 'vp', 'ks', 'vs', 'mkp', 'mvp', 'cbp', 'cbs', 'ccp', 'ccs', 'cfp', 'cfs')}
    for l in range(depth):
        vec = lambda a: a[l].reshape(1, -1)
        lw = {'w_out': w_out[l].astype(BF16), 'convb_w': convb_w[l], 'convb_b': vec(convb_b),
              'lnb_g': vec(lnb_g), 'lnb_b': vec(lnb_b), 'convc_w': convc_w[l], 'ln1_g': vec(ln1_g),
              'ln1_b': vec(ln1_b), 'w_qx': w_qx[l].astype(BF16), 'w_ox': w_ox[l].astype(BF16),
              'ln2_g': vec(ln2_g), 'ln2_b': vec(ln2_b), 'w_gate': w_gate[l].astype(BF16),
              'w_up': w_up[l].astype(BF16), 'convf_w': convf_w[l], 'convf_b': vec(convf_b),
              'w_down': w_down[l].astype(BF16), 'ln3_g': vec(ln3_g), 'ln3_b': vec(ln3_b)}
        w_in_bf = w_in[l].astype(BF16)

        qt, kb, vt, k, v, glu, uc, go, kmean = _in_proj(xp, w_in_bf, prompt=True)
        q_s, k_s, v_s, glu_s, uc_s, go_s = _in_proj(xs, w_in_bf, prompt=False)
        seq_major = lambda a: a.reshape(ds, bs, -1).transpose(1, 0, 2)
        q_sm, k_sm, v_sm = seq_major(q_s), seq_major(k_s), seq_major(v_s)
        pages_per_step = _fused_pages_per_step((H_A // 2) * (seq // MOBA_BLOCK), bs, page_table.shape[1])
        if pages_per_step is None:
            ya = _moba_prompt(rel_bias, qt, kb, vt, kmean)
            ya_s = _moba_sample(l, rel_bias, page_table, cache_k, cache_v, q_sm, k_sm, v_sm)
        else:
            ya, ya_s = _moba_fused(l, rel_bias, qt, kb, vt, kmean, page_table, cache_k, cache_v,
                                   q_sm, k_sm, v_sm, pages_per_step)

        mk, mv = _mem_proj(mem_p, w_kx[l].astype(BF16), w_vx[l].astype(BF16))
        x1 = _mix(xp, ya, glu, uc, go, zeros_b, zeros_c, lw, shift=1, alpha=alpha)
        x2 = _cross(x1, mk[None], mv[None], lw, alpha=alpha)
        xp, st_f = _ffn(x2, zeros_f, lw, shift=1, alpha=alpha)
        from_rows = lambda a: a.reshape(1, H_A, HD_A, seq).transpose(0, 3, 1, 2)
        outs['kp'].append(from_rows(k))
        outs['vp'].append(from_rows(v))
        outs['mkp'].append(mk.reshape(1, n_mem, H_X, HD_X))
        outs['mvp'].append(mv.reshape(1, n_mem, H_X, HD_X))
        outs['cbp'].append(glu[None, seq - (CONV_B - 1):])
        outs['ccp'].append(uc[None, seq - (CONV_C - 1):])
        outs['cfp'].append(st_f[None, st_f.shape[0] - (CONV_F - 1):])

        hb = state_conv_b[l].transpose(1, 0, 2)
        hc = state_conv_c[l].transpose(1, 0, 2)
        hf = state_conv_f[l].transpose(1, 0, 2)
        glu, uc, go = glu_s, uc_s, go_s
        ya = ya_s.transpose(1, 0, 2).reshape(ds * bs, W_A)
        x1 = _mix(xs, ya, glu, uc, go, hb.reshape(-1, W_B), hc.reshape(-1, W_C), lw, shift=bs, alpha=alpha)
        x2 = _cross(x1, mem_k_sample, mem_v_sample, lw, alpha=alpha, layer=l)
        xs, st_f = _ffn(x2, hf.reshape(-1, D_FF), lw, shift=bs, alpha=alpha)
        outs['ks'].append(k_sm.reshape(bs, ds, H_A, HD_A))
        outs['vs'].append(v_sm.reshape(bs, ds, H_A, HD_A))
        tail = lambda hist, new, n: jnp.concatenate([hist, new.reshape(ds, bs, -1)], axis=0)[-n:].transpose(1, 0, 2)
        outs['cbs'].append(tail(hb, glu, CONV_B - 1))
        outs['ccs'].append(tail(hc, uc, CONV_C - 1))
        outs['cfs'].append(st_f.reshape(CONV_F - 1, bs, D_FF).transpose(1, 0, 2))

    y_prompt = xp.reshape(1, seq, D_MODEL)
    y_sample = xs.reshape(ds, bs, D_MODEL).transpose(1, 0, 2)
    st = lambda n, axis: jnp.stack(outs[n], axis=axis)
    return (y_prompt, y_sample, st('kp', 1), st('vp', 1), st('ks', 1), st('vs', 1), st('mkp', 0), st('mvp', 0),
            st('cbp', 0), st('cbs', 0), st('ccp', 0), st('ccs', 0), st('cfp', 0), st('cfs', 0))
```

```python
import functools
import math

import numpy as np
import jax
import jax.numpy as jnp
from jax import lax
from jax.experimental import pallas as pl
from jax.experimental.pallas import tpu as pltpu

F32 = jnp.float32
BF16 = jnp.bfloat16

D_MODEL = 1024
PAGE_SIZE = 128
HD_A = 64
W_A = 512
H_A = 8
W_B = 256
W_C = 256
PROJ_IN = 2816
MOBA_BLOCK = 256
MOBA_TOPK = 3
CONV_B = 31
CONV_C = 3
CONV_F = 3
D_FF = 2816
FF_CHUNK = 256
H_X = 4
HD_X = 256
REL_BUCKETS = 32
REL_MAX_DIST = 128
LN_EPS = 1e-5
SUBLANES = 8
LANES = 128
MEM_ROWS = H_X * (HD_X // LANES)

NEG = -1e30
VROWS = 80
STEP_BLOCKS = 2
FAR_UNROLLS = (8, 4, 2)
LOG2E = math.log2(math.e)
V7X_VMEM_LIMIT = 56 * 1024 * 1024
ROW_TILE = 512
SAMPLE_PAGES_PER_STEP = 32


def _alpha(depth):
    return (2 * depth) ** 0.25


def _row_tile(t):
    tm = min(ROW_TILE, t)
    assert t % tm == 0 and (tm % SUBLANES == 0 or tm == t)
    return tm


def _cparams(sem):
    return pltpu.CompilerParams(dimension_semantics=sem, vmem_limit_bytes=V7X_VMEM_LIMIT)


def _full(shape):
    return pl.BlockSpec(shape, lambda *_: (0,) * len(shape))


def _sigmoid(x):
    return 1.0 / (1.0 + jnp.exp(-x))


def _layer_norm(h, g, b):
    mu = jnp.mean(h, axis=-1, keepdims=True)
    d = h - mu
    var = jnp.mean(d * d, axis=-1, keepdims=True)
    return d * lax.rsqrt(var + LN_EPS) * g + b


def _dot(a, b):
    return jnp.dot(a, b, preferred_element_type=F32)


def _dot_nt(a, b):
    return lax.dot_general(a, b, (((1,), (1,)), ((), ())), preferred_element_type=F32)


def _rel_bucket_np(dist):
    n = np.maximum(dist, 0)
    max_exact = REL_BUCKETS // 2
    nf = np.maximum(n, max_exact).astype(np.float32)
    large = max_exact + (np.log(nf / np.float32(max_exact)) / np.float32(math.log(REL_MAX_DIST / max_exact))
                         * np.float32(REL_BUCKETS - max_exact)).astype(np.int32)
    return np.where(n < max_exact, n, np.minimum(large, REL_BUCKETS - 1)).astype(np.int32)


def _in_proj_kernel(x_ref, w_ref, *outs, prompt, tm):
    xb = x_ref[...].astype(BF16)

    def proj(lo, hi):
        return _dot(xb, w_ref[:, lo:hi])

    q = proj(0, W_A) * (HD_A ** -0.5 * (LOG2E if prompt else 1.0))
    k = proj(W_A, 2 * W_A)
    v = proj(2 * W_A, 3 * W_A)
    o = 3 * W_A
    glu = proj(o, o + W_B) * _sigmoid(proj(o + W_B, o + 2 * W_B))
    o += 2 * W_B
    gate_out = proj(o, o + W_C)
    uc = proj(o + W_C, o + 2 * W_C) * proj(o + 2 * W_C, o + 3 * W_C)
    if prompt:
        qt_ref, kb_ref, vt_ref, k_ref, v_ref, glu_ref, uc_ref, go_ref, km_ref = outs
        for c in range(W_A // LANES):
            cols = slice(c * LANES, (c + 1) * LANES)
            qt_ref[cols, :] = q[:, cols].T.astype(BF16)
            k_ref[cols, :] = k[:, cols].T
            v_ref[cols, :] = v[:, cols].T
        kb_ref[...] = k.astype(BF16)
        for s in range(tm // MOBA_BLOCK):
            rows = slice(s * MOBA_BLOCK, (s + 1) * MOBA_BLOCK)
            for c in range(W_A // LANES):
                vt2 = v[rows, c * LANES:(c + 1) * LANES].T.astype(BF16)
                for hh in range(2):
                    r0 = (2 * c + hh) * VROWS
                    vt_ref[s, r0:r0 + HD_A, :] = vt2[hh * HD_A:(hh + 1) * HD_A, :]
                    vt_ref[s, r0 + HD_A:r0 + VROWS, :] = jnp.ones((VROWS - HD_A, MOBA_BLOCK), BF16)
            km_ref[s] = jnp.sum(k[rows, :], axis=0, keepdims=True) * (1.0 / MOBA_BLOCK)
    else:
        q_ref, k_ref, v_ref, glu_ref, uc_ref, go_ref = outs
        q_ref[...] = q
        k_ref[...] = k
        v_ref[...] = v
    glu_ref[...] = glu
    uc_ref[...] = uc
    go_ref[...] = gate_out


def _in_proj(x, w_bf, *, prompt):
    t = x.shape[0]
    tm = _row_tile(t)
    nb = t // MOBA_BLOCK
    row = lambda n: pl.BlockSpec((tm, n), lambda i: (i, 0))
    sd = jax.ShapeDtypeStruct
    if prompt:
        assert tm % MOBA_BLOCK == 0
        spb = tm // MOBA_BLOCK
        out_shape = [sd((W_A, t), BF16), sd((t, W_A), BF16), sd((nb, H_A * VROWS, MOBA_BLOCK), BF16),
                     sd((W_A, t), F32), sd((W_A, t), F32), sd((t, W_B), F32), sd((t, W_C), F32),
                     sd((t, W_C), F32), sd((nb, 1, W_A), F32)]
        col = pl.BlockSpec((W_A, tm), lambda i: (0, i))
        out_specs = [col, row(W_A),
                     pl.BlockSpec((spb, H_A * VROWS, MOBA_BLOCK), lambda i: (i, 0, 0)),
                     col, col, row(W_B), row(W_C), row(W_C),
                     pl.BlockSpec((spb, 1, W_A), lambda i: (i, 0, 0))]
    else:
        out_shape = [sd((t, W_A), F32), sd((t, W_A), F32), sd((t, W_A), F32), sd((t, W_B), F32),
                     sd((t, W_C), F32), sd((t, W_C), F32)]
        out_specs = [row(W_A), row(W_A), row(W_A), row(W_B), row(W_C), row(W_C)]
    return pl.pallas_call(
        functools.partial(_in_proj_kernel, prompt=prompt, tm=tm),
        grid=(t // tm,),
        in_specs=[row(D_MODEL), _full((D_MODEL, PROJ_IN))],
        out_specs=out_specs, out_shape=out_shape,
        compiler_params=_cparams(("arbitrary",)),
        name="in_proj_prompt" if prompt else "in_proj_sample",
    )(x, w_bf)


def _moba_prompt_kernel(*refs, nb):
    _moba_prompt_step(pl.program_id(0), pl.program_id(1), *refs, nb=nb)


def _moba_prompt_step(p, i, rb_ref, bko_ref, bkp_ref, qt_ref, k_ref, vt_ref, km_ref, o_ref,
                      bown, bprev, sel_ref, m_ref, acc_ref, sping, spong, *, nb):
    blk = MOBA_BLOCK

    @pl.when(i == 0)
    def _():
        bko = bko_ref[...]
        bkp = bkp_ref[...]
        for hh in range(2):
            h = 2 * p + hh
            far = rb_ref[REL_BUCKETS - 1, h]
            bo = jnp.full((blk, blk), NEG, F32)
            bp = jnp.zeros((blk, blk), F32)
            for b in range(REL_BUCKETS):
                val = (rb_ref[b, h] - far) * LOG2E
                bo = jnp.where(bko == b, val, bo)
                bp = jnp.where(bkp == b, val, bp)
            bown[hh] = bo
            bprev[hh] = bp

    qt = qt_ref[...]
    rowi = lax.broadcasted_iota(jnp.int32, qt.shape, 0)
    zero = jnp.zeros_like(qt)
    qts = [jnp.where(rowi < HD_A, qt, zero), jnp.where(rowi >= HD_A, qt, zero)]

    km = km_ref[...].astype(BF16)
    bidx = lax.broadcasted_iota(jnp.int32, (nb, blk), 0)
    for hh in range(2):
        g = _dot(km, qts[hh])
        g = jnp.where(bidx < i, g, -jnp.inf)
        selected = jnp.zeros((nb, blk), jnp.bool_)
        for _ in range(MOBA_TOPK):
            mx = jnp.max(g, axis=0, keepdims=True)
            first = jnp.min(jnp.where(g == mx, bidx, nb), axis=0, keepdims=True)
            hit = (bidx == first) & (mx > -jnp.inf)
            selected = selected | hit
            g = jnp.where(hit, -jnp.inf, g)
        sel_ref[hh] = jnp.where(selected, 0.0, NEG)
        m_ref[hh] = jnp.full((1, blk), NEG, F32)
        acc_ref[hh] = jnp.zeros((VROWS, blk), F32)

    def scores(j, hh):
        return _dot(k_ref[j], qts[hh])

    def softmax_pv(tiles, hh):
        mx = None
        for s, j, colbias in tiles:
            bm = jnp.max(s, axis=0, keepdims=True)
            if colbias is not None:
                bm = bm + colbias
            mx = bm if mx is None else jnp.maximum(mx, bm)
        m_old = m_ref[hh]
        m_new = jnp.maximum(m_old, mx)
        a = jnp.exp2(m_old - m_new)
        pts, vts = [], []
        for s, j, colbias in tiles:
            c = m_new if colbias is None else m_new - colbias
            pts.append(jnp.exp2(s - c).astype(BF16))
            vts.append(vt_ref[j, hh * VROWS:(hh + 1) * VROWS, :])
        r = _dot(jnp.concatenate(vts, axis=1), jnp.concatenate(pts, axis=0))
        acc_ref[hh] = a * acc_ref[hh] + r
        m_ref[hh] = m_new

    n_far = jnp.maximum(i - 1, 0)
    jp = jnp.maximum(i - 1, 0)
    s_own = [scores(i, hh) + bown[hh] for hh in range(2)]
    s_prev = [scores(jp, hh) + bprev[hh] for hh in range(2)]
    for hh in range(2):
        for d in range(STEP_BLOCKS):
            sping[hh, d] = scores(jnp.minimum(d, jnp.maximum(n_far - 1, 0)), hh)
    for hh in range(2):
        prev_mask = jnp.where(i >= 1, sel_ref[hh, pl.ds(jp, 1), :], NEG)
        softmax_pv([(s_own[hh], i, None), (s_prev[hh], jp, prev_mask)], hh)

    bufs = [sping, spong]

    def far_loop(first, unroll, trips):
        last = n_far - 1

        def trip(t, carry):
            for u in range(unroll):
                cur, nxt = bufs[u % 2], bufs[1 - u % 2]
                j = first + (t * unroll + u) * STEP_BLOCKS
                for hh in range(2):
                    for d in range(STEP_BLOCKS):
                        nxt[hh, d] = scores(jnp.minimum(j + STEP_BLOCKS + d, last), hh)
                for hh in range(2):
                    tiles = []
                    for d in range(STEP_BLOCKS):
                        jc = jnp.minimum(j + d, last)
                        mask = jnp.where(j + d <= last, sel_ref[hh, pl.ds(jc, 1), :], NEG)
                        tiles.append((cur[hh, d], jc, mask))
                    softmax_pv(tiles, hh)
            return carry

        lax.fori_loop(0, trips, trip, 0)

    first = 0
    for unroll in FAR_UNROLLS:
        per_trip = STEP_BLOCKS * unroll
        left = n_far - first
        trips = (left + per_trip - 1) // per_trip if unroll == FAR_UNROLLS[-1] else left // per_trip
        far_loop(first, unroll, trips)
        first = first + trips * per_trip

    outs = []
    for hh in range(2):
        acc = acc_ref[hh]
        outs.append(acc[0:HD_A, :] / acc[HD_A:HD_A + 1, :])
    o_ref[...] = jnp.concatenate(outs, axis=0).T


def _moba_prompt(rel_bias, qt, kb, vt, kmean):
    t = qt.shape[1]
    nb = t // MOBA_BLOCK
    blk = MOBA_BLOCK
    c = np.arange(blk)[:, None]
    r = np.arange(blk)[None, :]
    bk_own = np.where(c <= r, _rel_bucket_np(r - c), -1).astype(np.int32)
    bk_prev = _rel_bucket_np(blk + r - c)
    smem = pl.BlockSpec(memory_space=pltpu.SMEM)
    return pl.pallas_call(
        functools.partial(_moba_prompt_kernel, nb=nb),
        grid=(H_A // 2, nb),
        in_specs=[smem, _full((blk, blk)), _full((blk, blk)),
                  pl.BlockSpec((LANES, blk), lambda p, i: (p, i)),
                  pl.BlockSpec((nb, blk, LANES), lambda p, i: (0, 0, p)),
                  pl.BlockSpec((nb, 2 * VROWS, blk), lambda p, i: (0, p, 0)),
                  pl.BlockSpec((nb, LANES), lambda p, i: (0, p))],
        out_specs=pl.BlockSpec((blk, LANES), lambda p, i: (i, p)),
        out_shape=jax.ShapeDtypeStruct((t, W_A), F32),
        scratch_shapes=[pltpu.VMEM((2, blk, blk), F32), pltpu.VMEM((2, blk, blk), F32),
                        pltpu.VMEM((2, nb, blk), F32), pltpu.VMEM((2, 1, blk), F32),
                        pltpu.VMEM((2, VROWS, blk), F32),
                        pltpu.VMEM((2, STEP_BLOCKS, blk, blk), F32), pltpu.VMEM((2, STEP_BLOCKS, blk, blk), F32)],
        compiler_params=_cparams(("arbitrary", "arbitrary")),
        name="moba_prompt",
    )(rel_bias, jnp.asarray(bk_own), jnp.asarray(bk_prev), qt, kb.reshape(nb, blk, W_A), vt,
      kmean.reshape(nb, W_A))


def _head_rows(x, n_tok):
    lane_head = lax.broadcasted_iota(jnp.int32, (H_A, W_A), 1) // HD_A
    sub = lax.broadcasted_iota(jnp.int32, (H_A, W_A), 0)
    parts = [jnp.where(lane_head == sub, jnp.broadcast_to(x[t:t + 1, :], (H_A, W_A)), 0.0)
             for t in range(n_tok)]
    return jnp.concatenate(parts, axis=0)


def _moba_sample_k_kernel(pt_ref, q_ref, kn_ref, rbr_ref, bkl_ref, *rest, u, nchunk, ds):
    kpages = rest[:u]
    p_ref, st_ref = rest[u], rest[u + 1]
    qbd_ref, s_ref, ks_ref = rest[u + 2:]
    _sample_keys_step(pl.program_id(1), q_ref, kn_ref, rbr_ref, bkl_ref, kpages, [p_ref], [st_ref],
                      qbd_ref, s_ref, ks_ref, u=u, nchunk=nchunk, ds=ds)


def _sample_keys_step(c, q_ref, kn_ref, rbr_ref, bkl_ref, kpages, p_refs, st_refs, qbd_ref, s_ref, ks_ref,
                      *, u, nchunk, ds):
    nrow = ds * H_A
    bpc = u // 2
    nbp = nchunk * bpc

    @pl.when(c == 0)
    def _():
        qbd_ref[...] = _head_rows(q_ref[...], ds).astype(BF16)
        ks_ref[...] = jnp.zeros_like(ks_ref)

    qbd = qbd_ref[...]
    ks = ks_ref[...]
    lane_blk = lax.broadcasted_iota(jnp.int32, ks.shape, 1)
    for v in range(bpc):
        k0, k1 = kpages[2 * v][...], kpages[2 * v + 1][...]
        kk = jnp.concatenate([k0.astype(BF16), k1.astype(BF16)], axis=1)
        s_ref[c, :, v * MOBA_BLOCK:(v + 1) * MOBA_BLOCK] = _dot(qbd, kk)
        col = jnp.sum(k0 + k1, axis=1, keepdims=True)
        ks = jnp.where(lane_blk == c * bpc + v, col, ks)
    ks_ref[...] = ks

    @pl.when(c == nchunk - 1)
    def _():
        blk = MOBA_BLOCK
        km = (ks_ref[...] * (1.0 / blk)).astype(BF16)
        g = _dot(qbd, km)
        lane = lax.broadcasted_iota(jnp.int32, g.shape, 1)
        g = jnp.where(lane < nbp, g, -jnp.inf)
        selected = jnp.zeros(g.shape, jnp.bool_)
        for _ in range(min(MOBA_TOPK, nbp)):
            mx = jnp.max(g, axis=1, keepdims=True)
            first = jnp.min(jnp.where(g == mx, lane, nbp), axis=1, keepdims=True)
            hit = (lane == first) & (mx > -jnp.inf)
            selected = selected | hit
            g = jnp.where(hit, -jnp.inf, g)
        maskadd = jnp.where(selected, 0.0, NEG)

        rbr = rbr_ref[...]
        far = rbr[:, REL_BUCKETS - 1:REL_BUCKETS]
        bkl = bkl_ref[...]
        bias_last = jnp.zeros((nrow, blk), F32)
        for b in range(REL_BUCKETS):
            bias_last = jnp.where(bkl == b, rbr[:, b:b + 1], bias_last)

        trow = lax.broadcasted_iota(jnp.int32, (nrow, 1), 0) // H_A
        qf = qbd.astype(F32)
        knb = kn_ref[...].astype(BF16).astype(F32)
        own = []
        for cc in range(ds):
            sc = jnp.sum(qf * knb[cc:cc + 1, :], axis=1, keepdims=True)
            bias = jnp.zeros((nrow, 1), F32)
            for d in range(ds - cc):
                bias = jnp.where(trow - cc == d, rbr[:, d:d + 1], bias)
            own.append(jnp.where(trow >= cc, sc + bias, NEG))

        def logits(n):
            cn, off = n // bpc, (n % bpc) * blk
            b = bias_last if n == nbp - 1 else far
            return s_ref[cn, :, off:off + blk] + b + maskadd[:, n:n + 1]

        mt = logits(0)
        for n in range(1, nbp):
            mt = jnp.maximum(mt, logits(n))
        m = jnp.max(mt, axis=1, keepdims=True)
        for cc in range(ds):
            m = jnp.maximum(m, own[cc])
        lt = jnp.zeros((nrow, blk), F32)
        for n in range(nbp):
            cn, off = n // bpc, (n % bpc) * blk
            pn = jnp.exp(logits(n) - m)
            lt = lt + pn
            for p_ref in p_refs:
                p_ref[cn, :, off:off + blk] = pn.astype(BF16)
        l = jnp.sum(lt, axis=1, keepdims=True)
        lane128 = lax.broadcasted_iota(jnp.int32, (nrow, LANES), 1)
        stats = jnp.zeros((nrow, LANES), F32)
        for cc in range(ds):
            po = jnp.exp(own[cc] - m)
            l = l + po
            stats = jnp.where(lane128 == cc, po, stats)
        for st_ref in st_refs:
            st_ref[...] = jnp.where(lane128 == ds, l, stats)


def _moba_sample_v_kernel(pt_ref, p_ref, st_ref, vn_ref, *rest, u, nchunk, ds):
    vpages = rest[:u]
    o_ref, acc_ref = rest[u], rest[u + 1]
    _sample_values_step(pl.program_id(1), p_ref, st_ref, vn_ref, vpages, o_ref, acc_ref, u=u, nchunk=nchunk, ds=ds)


def _sample_values_step(c, p_ref, st_ref, vn_ref, vpages, o_ref, acc_ref, *, u, nchunk, ds):

    @pl.when(c == 0)
    def _():
        acc_ref[...] = jnp.zeros_like(acc_ref)

    acc = acc_ref[...]
    for v in range(u // 2):
        vv = jnp.concatenate([vpages[2 * v][...].astype(BF16), vpages[2 * v + 1][...].astype(BF16)], axis=1)
        acc = acc + _dot_nt(p_ref[:, v * MOBA_BLOCK:(v + 1) * MOBA_BLOCK], vv)
    acc_ref[...] = acc

    @pl.when(c == nchunk - 1)
    def _():
        a = acc_ref[...]
        st = st_ref[...]
        vnb = vn_ref[...].astype(BF16).astype(F32)
        for cc in range(ds):
            a = a + st[:, cc:cc + 1].astype(BF16).astype(F32) * vnb[cc:cc + 1, :]
        a = a / st[:, ds:ds + 1]
        lane_head = lax.broadcasted_iota(jnp.int32, (H_A, W_A), 1) // HD_A
        sub = lax.broadcasted_iota(jnp.int32, (H_A, W_A), 0)
        for t in range(ds):
            rows = a[t * H_A:(t + 1) * H_A, :]
            o_ref[t:t + 1, :] = jnp.sum(jnp.where(lane_head == sub, rows, 0.0), axis=0, keepdims=True)


def _page_view(cache):
    n_pool, depth = cache.shape[:2]
    return cache.transpose(0, 1, 3, 4, 2).reshape(n_pool, depth, W_A, PAGE_SIZE)


def _sample_tables(rel_bias, ds):
    nrow = ds * H_A
    rbr = jnp.tile(rel_bias.T, (ds, 1))
    trow = np.arange(nrow)[:, None] // H_A
    bk_last = _rel_bucket_np(MOBA_BLOCK + trow - np.arange(MOBA_BLOCK)[None, :])
    return rbr, jnp.asarray(bk_last)


def _check_sample_shapes(n_pages, ds, u):
    past_len = n_pages * PAGE_SIZE
    assert past_len % MOBA_BLOCK == 0 and ds <= REL_BUCKETS // 2
    assert n_pages % u == 0 and u % 2 == 0 and past_len // MOBA_BLOCK <= LANES


def _moba_sample_values(layer, page_table, cv, probs, stats, v_new, u):
    b, ds, _ = v_new.shape
    nchunk = page_table.shape[1] // u
    nrow = ds * H_A
    page_spec = lambda j: pl.BlockSpec((None, None, W_A, PAGE_SIZE),
                                       lambda bi, c, pt: (pt[bi, c * u + j], layer, 0, 0))
    per_seq = lambda r, n: pl.BlockSpec((None, r, n), lambda bi, c, pt: (bi, 0, 0))
    return pl.pallas_call(
        functools.partial(_moba_sample_v_kernel, u=u, nchunk=nchunk, ds=ds),
        grid_spec=pltpu.PrefetchScalarGridSpec(
            num_scalar_prefetch=1, grid=(b, nchunk),
            in_specs=[pl.BlockSpec((None, None, nrow, u * PAGE_SIZE), lambda bi, c, pt: (bi, c, 0, 0)),
                      per_seq(nrow, LANES), per_seq(ds, W_A)] + [page_spec(j) for j in range(u)],
            out_specs=per_seq(ds, W_A),
            scratch_shapes=[pltpu.VMEM((nrow, W_A), F32)]),
        out_shape=jax.ShapeDtypeStruct((b, ds, W_A), F32),
        compiler_params=_cparams(("arbitrary", "arbitrary")),
        name="moba_sample_values",
    )(page_table, probs, stats, v_new, *([cv] * u))


def _moba_sample(layer, rel_bias, page_table, cache_k, cache_v, q, k_new, v_new):
    b, ds, _ = q.shape
    n_pages = page_table.shape[1]
    u = min(SAMPLE_PAGES_PER_STEP, n_pages)
    _check_sample_shapes(n_pages, ds, u)
    nchunk = n_pages // u
    nrow = ds * H_A
    ck, cv = _page_view(cache_k), _page_view(cache_v)
    rbr, bk_last = _sample_tables(rel_bias, ds)

    def page_spec(j):
        return pl.BlockSpec((None, None, W_A, PAGE_SIZE),
                            lambda bi, c, pt: (pt[bi, c * u + j], layer, 0, 0))

    per_seq = lambda r, n: pl.BlockSpec((None, r, n), lambda bi, c, pt: (bi, 0, 0))
    const = lambda r, n: pl.BlockSpec((r, n), lambda bi, c, pt: (0, 0))
    probs, stats = pl.pallas_call(
        functools.partial(_moba_sample_k_kernel, u=u, nchunk=nchunk, ds=ds),
        grid_spec=pltpu.PrefetchScalarGridSpec(
            num_scalar_prefetch=1, grid=(b, nchunk),
            in_specs=[per_seq(ds, W_A), per_seq(ds, W_A), const(nrow, REL_BUCKETS), const(nrow, MOBA_BLOCK)]
                     + [page_spec(j) for j in range(u)],
            out_specs=[pl.BlockSpec((None, nchunk, nrow, u * PAGE_SIZE), lambda bi, c, pt: (bi, 0, 0, 0)),
                       per_seq(nrow, LANES)],
            scratch_shapes=[pltpu.VMEM((nrow, W_A), BF16), pltpu.VMEM((nchunk, nrow, u * PAGE_SIZE), F32),
                            pltpu.VMEM((W_A, LANES), F32)]),
        out_shape=[jax.ShapeDtypeStruct((b, nchunk, nrow, u * PAGE_SIZE), BF16),
                   jax.ShapeDtypeStruct((b, nrow, LANES), F32)],
        compiler_params=_cparams(("arbitrary", "arbitrary")),
        name="moba_sample_keys",
    )(page_table, q, k_new, rbr, bk_last, *([ck] * u))
    return _moba_sample_values(layer, page_table, cv, probs, stats, v_new, u)


def _moba_fused_kernel(pt_ref, rb_ref, bko_ref, bkp_ref, qt_ref, k_ref, vt_ref, km_ref,
                       q_ref, kn_ref, vn_ref, rbr_ref, bkl_ref, *rest, nb, u, nchunk, ds):
    kpages, vpages = rest[:u], rest[u:2 * u]
    o_ref, p_out, st_out, y_ref = rest[2 * u:2 * u + 4]
    (bown, bprev, sel_ref, m_ref, acc_ref, sping, spong,
     qbd_ref, s_ref, ks_ref, p_scr, st_scr, accv_ref) = rest[2 * u + 4:]
    p = pl.program_id(0)
    i = pl.program_id(1)
    step = p * nb + i
    b = step // nchunk
    c = step % nchunk
    par = b % 2
    _sample_keys_step(c, q_ref, kn_ref, rbr_ref, bkl_ref, kpages, [p_out, p_scr.at[par]],
                      [st_out, st_scr.at[par]], qbd_ref, s_ref, ks_ref, u=u, nchunk=nchunk, ds=ds)

    @pl.when(b >= 1)
    def _():
        _sample_values_step(c, p_scr.at[1 - par, c], st_scr.at[1 - par], vn_ref, vpages, y_ref, accv_ref,
                            u=u, nchunk=nchunk, ds=ds)

    _moba_prompt_step(p, i, rb_ref, bko_ref, bkp_ref, qt_ref, k_ref, vt_ref, km_ref, o_ref,
                      bown, bprev, sel_ref, m_ref, acc_ref, sping, spong, nb=nb)


def _fused_pages_per_step(n_steps, bs, n_pages):
    if bs < 2 or (bs * n_pages) % n_steps:
        return None
    u = bs * n_pages // n_steps
    if u < 2 or u % 2 or u > SAMPLE_PAGES_PER_STEP or n_pages % u:
        return None
    return u


def _moba_fused(layer, rel_bias, qt, kb, vt, kmean, page_table, cache_k, cache_v, q, k_new, v_new, u):
    t = qt.shape[1]
    nb = t // MOBA_BLOCK
    blk = MOBA_BLOCK
    bs, ds, _ = q.shape
    n_pages = page_table.shape[1]
    _check_sample_shapes(n_pages, ds, u)
    nchunk = n_pages // u
    assert bs * nchunk == (H_A // 2) * nb
    nrow = ds * H_A
    ck, cv = _page_view(cache_k), _page_view(cache_v)
    rbr, bk_last = _sample_tables(rel_bias, ds)
    c_ = np.arange(blk)[:, None]
    r_ = np.arange(blk)[None, :]
    bk_own = np.where(c_ <= r_, _rel_bucket_np(r_ - c_), -1).astype(np.int32)
    bk_prev = _rel_bucket_np(blk + r_ - c_)

    seq_k = lambda p, i: (p * nb + i) // nchunk
    seq_v = lambda p, i: jnp.maximum(seq_k(p, i) - 1, 0)
    chunk = lambda p, i: (p * nb + i) % nchunk
    const = lambda shape: pl.BlockSpec(shape, lambda p, i, pt: (0,) * len(shape))
    per_k = lambda r, n: pl.BlockSpec((None, r, n), lambda p, i, pt: (seq_k(p, i), 0, 0))
    per_v = lambda r, n: pl.BlockSpec((None, r, n), lambda p, i, pt: (seq_v(p, i), 0, 0))
    kpage = lambda j: pl.BlockSpec((None, None, W_A, PAGE_SIZE),
                                   lambda p, i, pt: (pt[seq_k(p, i), chunk(p, i) * u + j], layer, 0, 0))
    vpage = lambda j: pl.BlockSpec((None, None, W_A, PAGE_SIZE),
                                   lambda p, i, pt: (pt[seq_v(p, i), chunk(p, i) * u + j], layer, 0, 0))
    smem = pl.BlockSpec(memory_space=pltpu.SMEM)
    ya, probs, stats, ys = pl.pallas_call(
        functools.partial(_moba_fused_kernel, nb=nb, u=u, nchunk=nchunk, ds=ds),
        grid_spec=pltpu.PrefetchScalarGridSpec(
            num_scalar_prefetch=1, grid=(H_A // 2, nb),
            in_specs=[smem, const((blk, blk)), const((blk, blk)),
                      pl.BlockSpec((LANES, blk), lambda p, i, pt: (p, i)),
                      pl.BlockSpec((nb, blk, LANES), lambda p, i, pt: (0, 0, p)),
                      pl.BlockSpec((nb, 2 * VROWS, blk), lambda p, i, pt: (0, p, 0)),
                      pl.BlockSpec((nb, LANES), lambda p, i, pt: (0, p)),
                      per_k(ds, W_A), per_k(ds, W_A), per_v(ds, W_A), const((nrow, REL_BUCKETS)),
                      const((nrow, blk))]
                     + [kpage(j) for j in range(u)] + [vpage(j) for j in range(u)],
            out_specs=[pl.BlockSpec((blk, LANES), lambda p, i, pt: (i, p)),
                       pl.BlockSpec((None, nchunk, nrow, u * PAGE_SIZE), lambda p, i, pt: (seq_k(p, i), 0, 0, 0)),
                       per_k(nrow, LANES), per_v(ds, W_A)],
            scratch_shapes=[pltpu.VMEM((2, blk, blk), F32), pltpu.VMEM((2, blk, blk), F32),
                            pltpu.VMEM((2, nb, blk), F32), pltpu.VMEM((2, 1, blk), F32),
                            pltpu.VMEM((2, VROWS, blk), F32),
                            pltpu.VMEM((2, STEP_BLOCKS, blk, blk), F32), pltpu.VMEM((2, STEP_BLOCKS, blk, blk), F32),
                            pltpu.VMEM((nrow, W_A), BF16), pltpu.VMEM((nchunk, nrow, u * PAGE_SIZE), F32),
                            pltpu.VMEM((W_A, LANES), F32),
                            pltpu.VMEM((2, nchunk, nrow, u * PAGE_SIZE), BF16), pltpu.VMEM((2, nrow, LANES), F32),
                            pltpu.VMEM((nrow, W_A), F32)]),
        out_shape=[jax.ShapeDtypeStruct((t, W_A), F32),
                   jax.ShapeDtypeStruct((bs, nchunk, nrow, u * PAGE_SIZE), BF16),
                   jax.ShapeDtypeStruct((bs, nrow, LANES), F32),
                   jax.ShapeDtypeStruct((bs - 1, ds, W_A), F32)],
        compiler_params=_cparams(("arbitrary", "arbitrary")),
        name="moba_fused",
    )(page_table, rel_bias, jnp.asarray(bk_own), jnp.asarray(bk_prev), qt, kb.reshape(nb, blk, W_A), vt,
      kmean.reshape(nb, W_A), q, k_new, v_new, rbr, bk_last, *([ck] * u), *([cv] * u))
    y_last = _moba_sample_values(layer, page_table[bs - 1:], cv, probs[bs - 1:], stats[bs - 1:], v_new[bs - 1:], u)
    return ya, jnp.concatenate([ys, y_last], axis=0)


def _mix_kernel(x_ref, ya_ref, glu_ref, uc_ref, go_ref, hb_ref, hc_ref, cbw_ref, cbb_ref, lbg_ref, lbb_ref,
                ccw_ref, wo_ref, g1_ref, b1_ref, o_ref, extb, extc, shb, *, tm, shift, hbb, hbc, alpha):
    @pl.when(pl.program_id(0) == 0)
    def _():
        extb[0:hbb, :] = hb_ref[...]
        extc[0:hbc, :] = hc_ref[...]

    mix_a = _dot(ya_ref[...].astype(BF16), wo_ref[0:W_A, :])
    extb[hbb:hbb + tm, :] = glu_ref[...]
    extc[hbc:hbc + tm, :] = uc_ref[...]
    offb = hbb - (CONV_B - 1) * shift
    if shift != 1:
        tap = lambda w: extb[offb + w * shift:offb + w * shift + tm, :]
    else:
        for r in range(SUBLANES):
            rows = tm + SUBLANES * ((CONV_B - 1 - r) // SUBLANES)
            shb[r, 0:rows, :] = extb[offb + r:offb + r + rows, :]
        tap = lambda w: shb[w % SUBLANES, w - w % SUBLANES:w - w % SUBLANES + tm, :]
    cb = jnp.broadcast_to(cbb_ref[...], (tm, W_B))
    for w in range(CONV_B):
        cb = cb + tap(w) * cbw_ref[w:w + 1, :]
    yb = _layer_norm(cb, lbg_ref[...], lbb_ref[...])
    yb = yb * _sigmoid(yb)
    offc = hbc - (CONV_C - 1) * shift
    cc = extc[offc:offc + tm, :] * ccw_ref[0:1, :]
    for w in range(1, CONV_C):
        cc = cc + extc[offc + w * shift:offc + w * shift + tm, :] * ccw_ref[w:w + 1, :]
    yc = go_ref[...] * cc
    mix = (mix_a
           + _dot(yb.astype(BF16), wo_ref[W_A:W_A + W_B, :])
           + _dot(yc.astype(BF16), wo_ref[W_A + W_B:W_A + W_B + W_C, :]))
    o_ref[...] = _layer_norm(alpha * x_ref[...] + mix, g1_ref[...], b1_ref[...])
    nb_rows = extb[tm:tm + hbb, :]
    nc_rows = extc[tm:tm + hbc, :]
    extb[0:hbb, :] = nb_rows
    extc[0:hbc, :] = nc_rows


def _mix(x, ya, glu, uc, go, hist_b, hist_c, lw, *, shift, alpha):
    t = x.shape[0]
    tm = _row_tile(t)
    hbb, hbc = hist_b.shape[0], hist_c.shape[0]
    row = lambda n: pl.BlockSpec((tm, n), lambda i: (i, 0))
    return pl.pallas_call(
        functools.partial(_mix_kernel, tm=tm, shift=shift, hbb=hbb, hbc=hbc, alpha=alpha),
        grid=(t // tm,),
        in_specs=[row(D_MODEL), row(W_A), row(W_B), row(W_C), row(W_C), _full((hbb, W_B)), _full((hbc, W_C)),
                  _full((CONV_B, W_B)), _full((1, W_B)), _full((1, W_B)), _full((1, W_B)), _full((CONV_C, W_C)),
                  _full((D_MODEL, D_MODEL)), _full((1, D_MODEL)), _full((1, D_MODEL))],
        out_specs=row(D_MODEL),
        out_shape=jax.ShapeDtypeStruct((t, D_MODEL), F32),
        scratch_shapes=[pltpu.VMEM((hbb + tm, W_B), F32), pltpu.VMEM((hbc + tm, W_C), F32),
                        pltpu.VMEM((SUBLANES, tm + SUBLANES * ((CONV_B - 1) // SUBLANES), W_B), F32)],
        compiler_params=_cparams(("arbitrary",)),
        name="mix_ln1",
    )(x, ya, glu, uc, go, hist_b, hist_c, lw['convb_w'], lw['convb_b'], lw['lnb_g'], lw['lnb_b'],
      lw['convc_w'], lw['w_out'], lw['ln1_g'], lw['ln1_b'])


def _cross_kernel(x_ref, mk_ref, mv_ref, wq_ref, wo_ref, g_ref, b_ref, o_ref, qx_ref, acc_ref,
                  *, tm, n_groups, n_mem, interleaved, alpha):
    g = pl.program_id(1)

    @pl.when(g == 0)
    def _():
        qx_ref[...] = (_dot(x_ref[...].astype(BF16), wq_ref[...]) * (HD_X ** -0.5)).astype(BF16)
        acc_ref[...] = jnp.zeros_like(acc_ref)

    def head_mem(ref, h):
        if not interleaved:
            return ref[:, h * HD_X:(h + 1) * HD_X].astype(BF16)
        tiles = [ref[pl.ds(dt * H_X + h, n_mem, stride=MEM_ROWS), :] for dt in range(HD_X // LANES)]
        return jnp.concatenate(tiles, axis=1).astype(BF16)

    heads = []
    head_cols = [slice(h * HD_X, (h + 1) * HD_X) for h in range(H_X)]
    scores = [_dot_nt(qx_ref[:, head_cols[h]], head_mem(mk_ref, h)) for h in range(H_X)]
    for h, s in enumerate(scores):
        s = s - jnp.max(s, axis=-1, keepdims=True)
        e = jnp.exp(s)
        pr = e / jnp.sum(e, axis=-1, keepdims=True)
        heads.append(_dot(pr.astype(BF16), head_mem(mv_ref, h)))
    o = jnp.concatenate(heads, axis=-1)
    if n_groups == 1:
        acc_ref[...] = o
    else:
        rgrp = (pl.program_id(0) * tm + lax.broadcasted_iota(jnp.int32, (tm, 1), 0)) % n_groups
        acc_ref[...] = jnp.where(rgrp == g, o, acc_ref[...])

    @pl.when(g == n_groups - 1)
    def _():
        ox = _dot(acc_ref[...].astype(BF16), wo_ref[...])
        o_ref[...] = _layer_norm(alpha * x_ref[...] + ox, g_ref[...], b_ref[...])


def _interleave_mem(cache_mem):
    depth, b, n_mem = cache_mem.shape[:3]
    v = cache_mem.reshape(depth, b, n_mem, H_X, HD_X // LANES, LANES).transpose(0, 1, 2, 4, 3, 5)
    return v.reshape(depth, b, n_mem * MEM_ROWS, LANES)


def _cross(x, mem_k, mem_v, lw, *, alpha, layer=None):
    t = x.shape[0]
    tm = _row_tile(t)
    row = pl.BlockSpec((tm, D_MODEL), lambda i, g: (i, 0))
    interleaved = layer is not None
    if interleaved:
        n_groups, n_mem = mem_k.shape[1], mem_k.shape[2] // MEM_ROWS
        mem = pl.BlockSpec((None, None, n_mem * MEM_ROWS, LANES), lambda i, g: (layer, g, 0, 0))
    else:
        n_groups, n_mem, _ = mem_k.shape
        mem = pl.BlockSpec((None, n_mem, D_MODEL), lambda i, g: (g, 0, 0))
    cst = lambda r, n: pl.BlockSpec((r, n), lambda i, g: (0, 0))
    return pl.pallas_call(
        functools.partial(_cross_kernel, tm=tm, n_groups=n_groups, n_mem=n_mem, interleaved=interleaved,
                          alpha=alpha),
        grid=(t // tm, n_groups),
        in_specs=[row, mem, mem, cst(D_MODEL, D_MODEL), cst(D_MODEL, D_MODEL), cst(1, D_MODEL), cst(1, D_MODEL)],
        out_specs=row,
        out_shape=jax.ShapeDtypeStruct((t, D_MODEL), F32),
        scratch_shapes=[pltpu.VMEM((tm, D_MODEL), BF16), pltpu.VMEM((tm, D_MODEL), F32)],
        compiler_params=_cparams(("arbitrary", "arbitrary")),
        name="cross_ln2",
    )(x, mem_k, mem_v, lw['w_qx'], lw['w_ox'], lw['ln2_g'], lw['ln2_b'])


def _ffn_kernel(x_ref, hf_ref, wg_ref, wu_ref, cfw_ref, cfb_ref, wd_ref, g_ref, b_ref, o_ref, st_ref,
                carry, ext, acc_ref, *, tm, shift, hbf, alpha):
    @pl.when(pl.program_id(0) == 0)
    def _():
        carry[...] = hf_ref[...]

    xb = x_ref[...].astype(BF16)
    off = hbf - (CONV_F - 1) * shift
    n_chunks = D_FF // FF_CHUNK

    def gate_up(c):
        cols = slice(c * FF_CHUNK, (c + 1) * FF_CHUNK)
        return _dot(xb, wg_ref[:, cols]), _dot(xb, wu_ref[:, cols])

    nxt = gate_up(0)
    for c in range(n_chunks):
        cols = slice(c * FF_CHUNK, (c + 1) * FF_CHUNK)
        gate, up = nxt
        if c + 1 < n_chunks:
            nxt = gate_up(c + 1)
        ext[0:hbf, :] = carry[:, cols]
        ext[hbf:hbf + tm, :] = gate
        gf = jnp.broadcast_to(cfb_ref[:, cols], (tm, FF_CHUNK))
        for w in range(CONV_F):
            gf = gf + ext[off + w * shift:off + w * shift + tm, :] * cfw_ref[w:w + 1, cols]
        carry[:, cols] = ext[tm:tm + hbf, :]
        hid = gf * _sigmoid(gf) * up
        part = _dot(hid.astype(BF16), wd_ref[cols, :])
        if c == 0:
            acc_ref[...] = part
        else:
            acc_ref[...] += part
    o_ref[...] = _layer_norm(alpha * x_ref[...] + acc_ref[...], g_ref[...], b_ref[...])
    st_ref[...] = carry[...]


def _ffn(x, hist_f, lw, *, shift, alpha):
    t = x.shape[0]
    tm = _row_tile(t)
    hbf = hist_f.shape[0]
    row = pl.BlockSpec((tm, D_MODEL), lambda i: (i, 0))
    return pl.pallas_call(
        functools.partial(_ffn_kernel, tm=tm, shift=shift, hbf=hbf, alpha=alpha),
        grid=(t // tm,),
        in_specs=[row, _full((hbf, D_FF)), _full((D_MODEL, D_FF)), _full((D_MODEL, D_FF)),
                  _full((CONV_F, D_FF)), _full((1, D_FF)), _full((D_FF, D_MODEL)), _full((1, D_MODEL)),
                  _full((1, D_MODEL))],
        out_specs=[row, _full((hbf, D_FF))],
        out_shape=[jax.ShapeDtypeStruct((t, D_MODEL), F32), jax.ShapeDtypeStruct((hbf, D_FF), F32)],
        scratch_shapes=[pltpu.VMEM((hbf, D_FF), F32), pltpu.VMEM((hbf + tm, FF_CHUNK), F32),
                        pltpu.VMEM((tm, D_MODEL), F32)],
        compiler_params=_cparams(("arbitrary",)),
        name="ffn_ln3",
    )(x, hist_f, lw['w_gate'], lw['w_up'], lw['convf_w'], lw['convf_b'], lw['w_down'], lw['ln3_g'], lw['ln3_b'])


def _mem_proj_kernel(m_ref, wk_ref, wv_ref, k_ref, v_ref):
    mb = m_ref[...].astype(BF16)
    k_ref[...] = _dot(mb, wk_ref[...])
    v_ref[...] = _dot(mb, wv_ref[...])


def _mem_proj(mem, wk_bf, wv_bf):
    n = mem.shape[0]
    sd = jax.ShapeDtypeStruct((n, D_MODEL), F32)
    return pl.pallas_call(
        _mem_proj_kernel, out_shape=[sd, sd],
        compiler_params=pltpu.CompilerParams(vmem_limit_bytes=V7X_VMEM_LIMIT),
        name="mem_proj",
    )(mem, wk_bf, wv_bf)


def kernel(x_prompt, x_sample, mem_prompt, cache_k, cache_v, page_table, cache_mem_k, cache_mem_v, state_conv_b, state_conv_c, state_conv_f, rel_bias, w_in, w_out, convb_w, convb_b, lnb_g, lnb_b, convc_w, ln1_g, ln1_b, w_qx, w_kx, w_vx, w_ox, ln2_g, ln2_b, w_gate, w_up, convf_w, convf_b, w_down, ln3_g, ln3_b):
    bp, seq, _ = x_prompt.shape
    bs, ds, _ = x_sample.shape
    depth = w_in.shape[0]
    n_mem = mem_prompt.shape[1]
    alpha = _alpha(depth)
    assert bp == 1 and seq % MOBA_BLOCK == 0

    xp = x_prompt.reshape(seq, D_MODEL)
    xs = x_sample.transpose(1, 0, 2).reshape(ds * bs, D_MODEL)
    mem_p = mem_prompt.reshape(n_mem, D_MODEL)
    zeros_b = jnp.zeros((32, W_B), F32)
    zeros_c = jnp.zeros((8, W_C), F32)
    zeros_f = jnp.zeros((8, D_FF), F32)
    mem_k_sample = _interleave_mem(cache_mem_k)
    mem_v_sample = _interleave_mem(cache_mem_v)

    outs = {n: [] for n in ('kp', 'vp', 'ks', 'vs', 'mkp', 'mvp', 'cbp', 'cbs', 'ccp', 'ccs', 'cfp', 'cfs')}
    for l in range(depth):
        vec = lambda a: a[l].reshape(1, -1)
        lw = {'w_out': w_out[l].astype(BF16), 'convb_w': convb_w[l], 'convb_b': vec(convb_b),
              'lnb_g': vec(lnb_g), 'lnb_b': vec(lnb_b), 'convc_w': convc_w[l], 'ln1_g': vec(ln1_g),
              'ln1_b': vec(ln1_b), 'w_qx': w_qx[l].astype(BF16), 'w_ox': w_ox[l].astype(BF16),
              'ln2_g': vec(ln2_g), 'ln2_b': vec(ln2_b), 'w_gate': w_gate[l].astype(BF16),
              'w_up': w_up[l].astype(BF16), 'convf_w': convf_w[l], 'convf_b': vec(convf_b),
              'w_down': w_down[l].astype(BF16), 'ln3_g': vec(ln3_g), 'ln3_b': vec(ln3_b)}
        w_in_bf = w_in[l].astype(BF16)

        qt, kb, vt, k, v, glu, uc, go, kmean = _in_proj(xp, w_in_bf, prompt=True)
        q_s, k_s, v_s, glu_s, uc_s, go_s = _in_proj(xs, w_in_bf, prompt=False)
        seq_major = lambda a: a.reshape(ds, bs, -1).transpose(1, 0, 2)
        q_sm, k_sm, v_sm = seq_major(q_s), seq_major(k_s), seq_major(v_s)
        pages_per_step = _fused_pages_per_step((H_A // 2) * (seq // MOBA_BLOCK), bs, page_table.shape[1])
        if pages_per_step is None:
            ya = _moba_prompt(rel_bias, qt, kb, vt, kmean)
            ya_s = _moba_sample(l, rel_bias, page_table, cache_k, cache_v, q_sm, k_sm, v_sm)
        else:
            ya, ya_s = _moba_fused(l, rel_bias, qt, kb, vt, kmean, page_table, cache_k, cache_v,
                                   q_sm, k_sm, v_sm, pages_per_step)

        mk, mv = _mem_proj(mem_p, w_kx[l].astype(BF16), w_vx[l].astype(BF16))
        x1 = _mix(xp, ya, glu, uc, go, zeros_b, zeros_c, lw, shift=1, alpha=alpha)
        x2 = _cross(x1, mk[None], mv[None], lw, alpha=alpha)
        xp, st_f = _ffn(x2, zeros_f, lw, shift=1, alpha=alpha)
        from_rows = lambda a: a.reshape(1, H_A, HD_A, seq).transpose(0, 3, 1, 2)
        outs['kp'].append(from_rows(k))
        outs['vp'].append(from_rows(v))
        outs['mkp'].append(mk.reshape(1, n_mem, H_X, HD_X))
        outs['mvp'].append(mv.reshape(1, n_mem, H_X, HD_X))
        outs['cbp'].append(glu[None, seq - (CONV_B - 1):])
        outs['ccp'].append(uc[None, seq - (CONV_C - 1):])
        outs['cfp'].append(st_f[None, st_f.shape[0] - (CONV_F - 1):])

        hb = state_conv_b[l].transpose(1, 0, 2)
        hc = state_conv_c[l].transpose(1, 0, 2)
        hf = state_conv_f[l].transpose(1, 0, 2)
        glu, uc, go = glu_s, uc_s, go_s
        ya = ya_s.transpose(1, 0, 2).reshape(ds * bs, W_A)
        x1 = _mix(xs, ya, glu, uc, go, hb.reshape(-1, W_B), hc.reshape(-1, W_C), lw, shift=bs, alpha=alpha)
        x2 = _cross(x1, mem_k_sample, mem_v_sample, lw, alpha=alpha, layer=l)
        xs, st_f = _ffn(x2, hf.reshape(-1, D_FF), lw, shift=bs, alpha=alpha)
        outs['ks'].append(k_sm.reshape(bs, ds, H_A, HD_A))
        outs['vs'].append(v_sm.reshape(bs, ds, H_A, HD_A))
        tail = lambda hist, new, n: jnp.concatenate([hist, new.reshape(ds, bs, -1)], axis=0)[-n:].transpose(1, 0, 2)
        outs['cbs'].append(tail(hb, glu, CONV_B - 1))
        outs['ccs'].append(tail(hc, uc, CONV_C - 1))
        outs['cfs'].append(st_f.reshape(CONV_F - 1, bs, D_FF).transpose(1, 0, 2))

    y_prompt = xp.reshape(1, seq, D_MODEL)
    y_sample = xs.reshape(ds, bs, D_MODEL).transpose(1, 0, 2)
    st = lambda n, axis: jnp.stack(outs[n], axis=axis)
    return (y_prompt, y_sample, st('kp', 1), st('vp', 1), st('ks', 1), st('vs', 1), st('mkp', 0), st('mvp', 0),
            st('cbp', 0), st('cbs', 0), st('ccp', 0), st('ccs', 0), st('cfp', 0), st('cfs', 0))
```

```python
import functools
import math

import numpy as np
import jax
import jax.numpy as jnp
from jax import lax
from jax.experimental import pallas as pl
from jax.experimental.pallas import tpu as pltpu

F32 = jnp.float32
BF16 = jnp.bfloat16

D_MODEL = 1024
PAGE_SIZE = 128
HD_A = 64
W_A = 512
H_A = 8
W_B = 256
W_C = 256
PROJ_IN = 2816
MOBA_BLOCK = 256
MOBA_TOPK = 3
CONV_B = 31
CONV_C = 3
CONV_F = 3
D_FF = 2816
FF_CHUNK = 256
H_X = 4
HD_X = 256
REL_BUCKETS = 32
REL_MAX_DIST = 128
LN_EPS = 1e-5
SUBLANES = 8
LANES = 128
MEM_ROWS = H_X * (HD_X // LANES)

NEG = -1e30
VROWS = 80
STEP_BLOCKS = 2
FAR_UNROLLS = (8, 4, 2)
LOG2E = math.log2(math.e)
V7X_VMEM_LIMIT = 56 * 1024 * 1024
ROW_TILE = 512
SAMPLE_PAGES_PER_STEP = 32


def _alpha(depth):
    return (2 * depth) ** 0.25


def _row_tile(t):
    tm = min(ROW_TILE, t)
    assert t % tm == 0 and (tm % SUBLANES == 0 or tm == t)
    return tm


def _cparams(sem):
    return pltpu.CompilerParams(dimension_semantics=sem, vmem_limit_bytes=V7X_VMEM_LIMIT)


def _full(shape):
    return pl.BlockSpec(shape, lambda *_: (0,) * len(shape))


def _sigmoid(x):
    return 1.0 / (1.0 + jnp.exp(-x))


def _layer_norm(h, g, b):
    mu = jnp.mean(h, axis=-1, keepdims=True)
    d = h - mu
    var = jnp.mean(d * d, axis=-1, keepdims=True)
    return d * lax.rsqrt(var + LN_EPS) * g + b


def _dot(a, b):
    return jnp.dot(a, b, preferred_element_type=F32)


def _dot_nt(a, b):
    return lax.dot_general(a, b, (((1,), (1,)), ((), ())), preferred_element_type=F32)


def _rel_bucket_np(dist):
    n = np.maximum(dist, 0)
    max_exact = REL_BUCKETS // 2
    nf = np.maximum(n, max_exact).astype(np.float32)
    large = max_exact + (np.log(nf / np.float32(max_exact)) / np.float32(math.log(REL_MAX_DIST / max_exact))
                         * np.float32(REL_BUCKETS - max_exact)).astype(np.int32)
    return np.where(n < max_exact, n, np.minimum(large, REL_BUCKETS - 1)).astype(np.int32)


def _in_proj_kernel(x_ref, w_ref, *outs, prompt, tm):
    xb = x_ref[...].astype(BF16)

    def proj(lo, hi):
        return _dot(xb, w_ref[:, lo:hi])

    q = proj(0, W_A) * (HD_A ** -0.5 * (LOG2E if prompt else 1.0))
    k = proj(W_A, 2 * W_A)
    v = proj(2 * W_A, 3 * W_A)
    o = 3 * W_A
    glu = proj(o, o + W_B) * _sigmoid(proj(o + W_B, o + 2 * W_B))
    o += 2 * W_B
    gate_out = proj(o, o + W_C)
    uc = proj(o + W_C, o + 2 * W_C) * proj(o + 2 * W_C, o + 3 * W_C)
    if prompt:
        qt_ref, kb_ref, vt_ref, k_ref, v_ref, glu_ref, uc_ref, go_ref, km_ref = outs
        for c in range(W_A // LANES):
            cols = slice(c * LANES, (c + 1) * LANES)
            qt_ref[cols, :] = q[:, cols].T.astype(BF16)
            k_ref[cols, :] = k[:, cols].T
            v_ref[cols, :] = v[:, cols].T
        kb_ref[...] = k.astype(BF16)
        for s in range(tm // MOBA_BLOCK):
            rows = slice(s * MOBA_BLOCK, (s + 1) * MOBA_BLOCK)
            for c in range(W_A // LANES):
                vt2 = v[rows, c * LANES:(c + 1) * LANES].T.astype(BF16)
                for hh in range(2):
                    r0 = (2 * c + hh) * VROWS
                    vt_ref[s, r0:r0 + HD_A, :] = vt2[hh * HD_A:(hh + 1) * HD_A, :]
                    vt_ref[s, r0 + HD_A:r0 + VROWS, :] = jnp.ones((VROWS - HD_A, MOBA_BLOCK), BF16)
            km_ref[s] = jnp.sum(k[rows, :], axis=0, keepdims=True) * (1.0 / MOBA_BLOCK)
    else:
        q_ref, k_ref, v_ref, glu_ref, uc_ref, go_ref = outs
        q_ref[...] = q
        k_ref[...] = k
        v_ref[...] = v
    glu_ref[...] = glu
    uc_ref[...] = uc
    go_ref[...] = gate_out


def _in_proj(x, w_bf, *, prompt):
    t = x.shape[0]
    tm = _row_tile(t)
    nb = t // MOBA_BLOCK
    row = lambda n: pl.BlockSpec((tm, n), lambda i: (i, 0))
    sd = jax.ShapeDtypeStruct
    if prompt:
        assert tm % MOBA_BLOCK == 0
        spb = tm // MOBA_BLOCK
        out_shape = [sd((W_A, t), BF16), sd((t, W_A), BF16), sd((nb, H_A * VROWS, MOBA_BLOCK), BF16),
                     sd((W_A, t), F32), sd((W_A, t), F32), sd((t, W_B), F32), sd((t, W_C), F32),
                     sd((t, W_C), F32), sd((nb, 1, W_A), F32)]
        col = pl.BlockSpec((W_A, tm), lambda i: (0, i))
        out_specs = [col, row(W_A),
                     pl.BlockSpec((spb, H_A * VROWS, MOBA_BLOCK), lambda i: (i, 0, 0)),
                     col, col, row(W_B), row(W_C), row(W_C),
                     pl.BlockSpec((spb, 1, W_A), lambda i: (i, 0, 0))]
    else:
        out_shape = [sd((t, W_A), F32), sd((t, W_A), F32), sd((t, W_A), F32), sd((t, W_B), F32),
                     sd((t, W_C), F32), sd((t, W_C), F32)]
        out_specs = [row(W_A), row(W_A), row(W_A), row(W_B), row(W_C), row(W_C)]
    return pl.pallas_call(
        functools.partial(_in_proj_kernel, prompt=prompt, tm=tm),
        grid=(t // tm,),
        in_specs=[row(D_MODEL), _full((D_MODEL, PROJ_IN))],
        out_specs=out_specs, out_shape=out_shape,
        compiler_params=_cparams(("arbitrary",)),
        name="in_proj_prompt" if prompt else "in_proj_sample",
    )(x, w_bf)


def _moba_prompt_kernel(*refs, nb):
    _moba_prompt_step(pl.program_id(0), pl.program_id(1), *refs, nb=nb)


def _moba_prompt_step(p, i, rb_ref, bko_ref, bkp_ref, qt_ref, k_ref, vt_ref, km_ref, o_ref,
                      bown, bprev, sel_ref, m_ref, acc_ref, sping, spong, *, nb):
    blk = MOBA_BLOCK

    @pl.when(i == 0)
    def _():
        bko = bko_ref[...]
        bkp = bkp_ref[...]
        for hh in range(2):
            h = 2 * p + hh
            far = rb_ref[REL_BUCKETS - 1, h]
            bo = jnp.full((blk, blk), NEG, F32)
            bp = jnp.zeros((blk, blk), F32)
            for b in range(REL_BUCKETS):
                val = (rb_ref[b, h] - far) * LOG2E
                bo = jnp.where(bko == b, val, bo)
                bp = jnp.where(bkp == b, val, bp)
            bown[hh] = bo
            bprev[hh] = bp

    qt = qt_ref[...]
    rowi = lax.broadcasted_iota(jnp.int32, qt.shape, 0)
    zero = jnp.zeros_like(qt)
    qts = [jnp.where(rowi < HD_A, qt, zero), jnp.where(rowi >= HD_A, qt, zero)]

    km = km_ref[...].astype(BF16)
    bidx = lax.broadcasted_iota(jnp.int32, (nb, blk), 0)
    for hh in range(2):
        g = _dot(km, qts[hh])
        g = jnp.where(bidx < i, g, -jnp.inf)
        selected = jnp.zeros((nb, blk), jnp.bool_)
        for _ in range(MOBA_TOPK):
            mx = jnp.max(g, axis=0, keepdims=True)
            first = jnp.min(jnp.where(g == mx, bidx, nb), axis=0, keepdims=True)
            hit = (bidx == first) & (mx > -jnp.inf)
            selected = selected | hit
            g = jnp.where(hit, -jnp.inf, g)
        sel_ref[hh] = jnp.where(selected, 0.0, NEG)
        m_ref[hh] = jnp.full((1, blk), NEG, F32)
        acc_ref[hh] = jnp.zeros((VROWS, blk), F32)

    def scores(j, hh):
        return _dot(k_ref[j], qts[hh])

    def softmax_pv(tiles, hh):
        mx = None
        for s, j, colbias in tiles:
            bm = jnp.max(s, axis=0, keepdims=True)
            if colbias is not None:
                bm = bm + colbias
            mx = bm if mx is None else jnp.maximum(mx, bm)
        m_old = m_ref[hh]
        m_new = jnp.maximum(m_old, mx)
        a = jnp.exp2(m_old - m_new)
        pts, vts = [], []
        for s, j, colbias in tiles:
            c = m_new if colbias is None else m_new - colbias
            pts.append(jnp.exp2(s - c).astype(BF16))
            vts.append(vt_ref[j, hh * VROWS:(hh + 1) * VROWS, :])
        r = _dot(jnp.concatenate(vts, axis=1), jnp.concatenate(pts, axis=0))
        acc_ref[hh] = a * acc_ref[hh] + r
        m_ref[hh] = m_new

    n_far = jnp.maximum(i - 1, 0)
    jp = jnp.maximum(i - 1, 0)
    s_own = [scores(i, hh) + bown[hh] for hh in range(2)]
    s_prev = [scores(jp, hh) + bprev[hh] for hh in range(2)]
    for hh in range(2):
        for d in range(STEP_BLOCKS):
            sping[hh, d] = scores(jnp.minimum(d, jnp.maximum(n_far - 1, 0)), hh)
    for hh in range(2):
        prev_mask = jnp.where(i >= 1, sel_ref[hh, pl.ds(jp, 1), :], NEG)
        softmax_pv([(s_own[hh], i, None), (s_prev[hh], jp, prev_mask)], hh)

    bufs = [sping, spong]

    def far_loop(first, unroll, trips):
        last = n_far - 1

        def trip(t, carry):
            for u in range(unroll):
                cur, nxt = bufs[u % 2], bufs[1 - u % 2]
                j = first + (t * unroll + u) * STEP_BLOCKS
                for hh in range(2):
                    for d in range(STEP_BLOCKS):
                        nxt[hh, d] = scores(jnp.minimum(j + STEP_BLOCKS + d, last), hh)
                for hh in range(2):
                    tiles = []
                    for d in range(STEP_BLOCKS):
                        jc = jnp.minimum(j + d, last)
                        mask = jnp.where(j + d <= last, sel_ref[hh, pl.ds(jc, 1), :], NEG)
                        tiles.append((cur[hh, d], jc, mask))
                    softmax_pv(tiles, hh)
            return carry

        lax.fori_loop(0, trips, trip, 0)

    first = 0
    for unroll in FAR_UNROLLS:
        per_trip = STEP_BLOCKS * unroll
        left = n_far - first
        trips = (left + per_trip - 1) // per_trip if unroll == FAR_UNROLLS[-1] else left // per_trip
        far_loop(first, unroll, trips)
        first = first + trips * per_trip

    outs = []
    for hh in range(2):
        acc = acc_ref[hh]
        outs.append(acc[0:HD_A, :] / acc[HD_A:HD_A + 1, :])
    o_ref[...] = jnp.concatenate(outs, axis=0).T


def _moba_prompt(rel_bias, qt, kb, vt, kmean):
    t = qt.shape[1]
    nb = t // MOBA_BLOCK
    blk = MOBA_BLOCK
    c = np.arange(blk)[:, None]
    r = np.arange(blk)[None, :]
    bk_own = np.where(c <= r, _rel_bucket_np(r - c), -1).astype(np.int32)
    bk_prev = _rel_bucket_np(blk + r - c)
    smem = pl.BlockSpec(memory_space=pltpu.SMEM)
    return pl.pallas_call(
        functools.partial(_moba_prompt_kernel, nb=nb),
        grid=(H_A // 2, nb),
        in_specs=[smem, _full((blk, blk)), _full((blk, blk)),
                  pl.BlockSpec((LANES, blk), lambda p, i: (p, i)),
                  pl.BlockSpec((nb, blk, LANES), lambda p, i: (0, 0, p)),
                  pl.BlockSpec((nb, 2 * VROWS, blk), lambda p, i: (0, p, 0)),
                  pl.BlockSpec((nb, LANES), lambda p, i: (0, p))],
        out_specs=pl.BlockSpec((blk, LANES), lambda p, i: (i, p)),
        out_shape=jax.ShapeDtypeStruct((t, W_A), F32),
        scratch_shapes=[pltpu.VMEM((2, blk, blk), F32), pltpu.VMEM((2, blk, blk), F32),
                        pltpu.VMEM((2, nb, blk), F32), pltpu.VMEM((2, 1, blk), F32),
                        pltpu.VMEM((2, VROWS, blk), F32),
                        pltpu.VMEM((2, STEP_BLOCKS, blk, blk), F32), pltpu.VMEM((2, STEP_BLOCKS, blk, blk), F32)],
        compiler_params=_cparams(("arbitrary", "arbitrary")),
        name="moba_prompt",
    )(rel_bias, jnp.asarray(bk_own), jnp.asarray(bk_prev), qt, kb.reshape(nb, blk, W_A), vt,
      kmean.reshape(nb, W_A))


def _head_rows(x, n_tok):
    lane_head = lax.broadcasted_iota(jnp.int32, (H_A, W_A), 1) // HD_A
    sub = lax.broadcasted_iota(jnp.int32, (H_A, W_A), 0)
    parts = [jnp.where(lane_head == sub, jnp.broadcast_to(x[t:t + 1, :], (H_A, W_A)), 0.0)
             for t in range(n_tok)]
    return jnp.concatenate(parts, axis=0)


def _moba_sample_k_kernel(pt_ref, q_ref, kn_ref, rbr_ref, bkl_ref, *rest, u, nchunk, ds):
    kpages = rest[:u]
    p_ref, st_ref = rest[u], rest[u + 1]
    qbd_ref, s_ref, ks_ref = rest[u + 2:]
    _sample_keys_step(pl.program_id(1), q_ref, kn_ref, rbr_ref, bkl_ref, kpages, [p_ref], [st_ref],
                      qbd_ref, s_ref, ks_ref, u=u, nchunk=nchunk, ds=ds)


def _sample_keys_step(c, q_ref, kn_ref, rbr_ref, bkl_ref, kpages, p_refs, st_refs, qbd_ref, s_ref, ks_ref,
                      *, u, nchunk, ds):
    nrow = ds * H_A
    bpc = u // 2
    nbp = nchunk * bpc

    @pl.when(c == 0)
    def _():
        qbd_ref[...] = _head_rows(q_ref[...], ds).astype(BF16)
        ks_ref[...] = jnp.zeros_like(ks_ref)

    qbd = qbd_ref[...]
    ks = ks_ref[...]
    lane_blk = lax.broadcasted_iota(jnp.int32, ks.shape, 1)
    for v in range(bpc):
        k0, k1 = kpages[2 * v][...], kpages[2 * v + 1][...]
        kk = jnp.concatenate([k0.astype(BF16), k1.astype(BF16)], axis=1)
        s_ref[c, :, v * MOBA_BLOCK:(v + 1) * MOBA_BLOCK] = _dot(qbd, kk)
        col = jnp.sum(k0 + k1, axis=1, keepdims=True)
        ks = jnp.where(lane_blk == c * bpc + v, col, ks)
    ks_ref[...] = ks

    @pl.when(c == nchunk - 1)
    def _():
        blk = MOBA_BLOCK
        km = (ks_ref[...] * (1.0 / blk)).astype(BF16)
        g = _dot(qbd, km)
        lane = lax.broadcasted_iota(jnp.int32, g.shape, 1)
        g = jnp.where(lane < nbp, g, -jnp.inf)
        selected = jnp.zeros(g.shape, jnp.bool_)
        for _ in range(min(MOBA_TOPK, nbp)):
            mx = jnp.max(g, axis=1, keepdims=True)
            first = jnp.min(jnp.where(g == mx, lane, nbp), axis=1, keepdims=True)
            hit = (lane == first) & (mx > -jnp.inf)
            selected = selected | hit
            g = jnp.where(hit, -jnp.inf, g)
        maskadd = jnp.where(selected, 0.0, NEG)

        rbr = rbr_ref[...]
        far = rbr[:, REL_BUCKETS - 1:REL_BUCKETS]
        bkl = bkl_ref[...]
        bias_last = jnp.zeros((nrow, blk), F32)
        for b in range(REL_BUCKETS):
            bias_last = jnp.where(bkl == b, rbr[:, b:b + 1], bias_last)

        trow = lax.broadcasted_iota(jnp.int32, (nrow, 1), 0) // H_A
        qf = qbd.astype(F32)
        knb = kn_ref[...].astype(BF16).astype(F32)
        own = []
        for cc in range(ds):
            sc = jnp.sum(qf * knb[cc:cc + 1, :], axis=1, keepdims=True)
            bias = jnp.zeros((nrow, 1), F32)
            for d in range(ds - cc):
                bias = jnp.where(trow - cc == d, rbr[:, d:d + 1], bias)
            own.append(jnp.where(trow >= cc, sc + bias, NEG))

        def logits(n):
            cn, off = n // bpc, (n % bpc) * blk
            b = bias_last if n == nbp - 1 else far
            return s_ref[cn, :, off:off + blk] + b + maskadd[:, n:n + 1]

        mt = logits(0)
        for n in range(1, nbp):
            mt = jnp.maximum(mt, logits(n))
        m = jnp.max(mt, axis=1, keepdims=True)
        for cc in range(ds):
            m = jnp.maximum(m, own[cc])
        lt = jnp.zeros((nrow, blk), F32)
        for n in range(nbp):
            cn, off = n // bpc, (n % bpc) * blk
            pn = jnp.exp(logits(n) - m)
            lt = lt + pn
            for p_ref in p_refs:
                p_ref[cn, :, off:off + blk] = pn.astype(BF16)
        l = jnp.sum(lt, axis=1, keepdims=True)
        lane128 = lax.broadcasted_iota(jnp.int32, (nrow, LANES), 1)
        stats = jnp.zeros((nrow, LANES), F32)
        for cc in range(ds):
            po = jnp.exp(own[cc] - m)
            l = l + po
            stats = jnp.where(lane128 == cc, po, stats)
        for st_ref in st_refs:
            st_ref[...] = jnp.where(lane128 == ds, l, stats)


def _moba_sample_v_kernel(pt_ref, p_ref, st_ref, vn_ref, *rest, u, nchunk, ds):
    vpages = rest[:u]
    o_ref, acc_ref = rest[u], rest[u + 1]
    _sample_values_step(pl.program_id(1), p_ref, st_ref, vn_ref, vpages, o_ref, acc_ref, u=u, nchunk=nchunk, ds=ds)


def _sample_values_step(c, p_ref, st_ref, vn_ref, vpages, o_ref, acc_ref, *, u, nchunk, ds):

    @pl.when(c == 0)
    def _():
        acc_ref[...] = jnp.zeros_like(acc_ref)

    acc = acc_ref[...]
    for v in range(u // 2):
        vv = jnp.concatenate([vpages[2 * v][...].astype(BF16), vpages[2 * v + 1][...].astype(BF16)], axis=1)
        acc = acc + _dot_nt(p_ref[:, v * MOBA_BLOCK:(v + 1) * MOBA_BLOCK], vv)
    acc_ref[...] = acc

    @pl.when(c == nchunk - 1)
    def _():
        a = acc_ref[...]
        st = st_ref[...]
        vnb = vn_ref[...].astype(BF16).astype(F32)
        for cc in range(ds):
            a = a + st[:, cc:cc + 1].astype(BF16).astype(F32) * vnb[cc:cc + 1, :]
        a = a / st[:, ds:ds + 1]
        lane_head = lax.broadcasted_iota(jnp.int32, (H_A, W_A), 1) // HD_A
        sub = lax.broadcasted_iota(jnp.int32, (H_A, W_A), 0)
        for t in range(ds):
            rows = a[t * H_A:(t + 1) * H_A, :]
            o_ref[t:t + 1, :] = jnp.sum(jnp.where(lane_head == sub, rows, 0.0), axis=0, keepdims=True)


def _page_view(cache):
    n_pool, depth = cache.shape[:2]
    return cache.transpose(0, 1, 3, 4, 2).reshape(n_pool, depth, W_A, PAGE_SIZE)


def _sample_tables(rel_bias, ds):
    nrow = ds * H_A
    rbr = jnp.tile(rel_bias.T, (ds, 1))
    trow = np.arange(nrow)[:, None] // H_A
    bk_last = _rel_bucket_np(MOBA_BLOCK + trow - np.arange(MOBA_BLOCK)[None, :])
    return rbr, jnp.asarray(bk_last)


def _check_sample_shapes(n_pages, ds, u):
    past_len = n_pages * PAGE_SIZE
    assert past_len % MOBA_BLOCK == 0 and ds <= REL_BUCKETS // 2
    assert n_pages % u == 0 and u % 2 == 0 and past_len // MOBA_BLOCK <= LANES


def _moba_sample_values(layer, page_table, cv, probs, stats, v_new, u):
    b, ds, _ = v_new.shape
    nchunk = page_table.shape[1] // u
    nrow = ds * H_A
    page_spec = lambda j: pl.BlockSpec((None, None, W_A, PAGE_SIZE),
                                       lambda bi, c, pt: (pt[bi, c * u + j], layer, 0, 0))
    per_seq = lambda r, n: pl.BlockSpec((None, r, n), lambda bi, c, pt: (bi, 0, 0))
    return pl.pallas_call(
        functools.partial(_moba_sample_v_kernel, u=u, nchunk=nchunk, ds=ds),
        grid_spec=pltpu.PrefetchScalarGridSpec(
            num_scalar_prefetch=1, grid=(b, nchunk),
            in_specs=[pl.BlockSpec((None, None, nrow, u * PAGE_SIZE), lambda bi, c, pt: (bi, c, 0, 0)),
                      per_seq(nrow, LANES), per_seq(ds, W_A)] + [page_spec(j) for j in range(u)],
            out_specs=per_seq(ds, W_A),
            scratch_shapes=[pltpu.VMEM((nrow, W_A), F32)]),
        out_shape=jax.ShapeDtypeStruct((b, ds, W_A), F32),
        compiler_params=_cparams(("arbitrary", "arbitrary")),
        name="moba_sample_values",
    )(page_table, probs, stats, v_new, *([cv] * u))


def _moba_sample(layer, rel_bias, page_table, cache_k, cache_v, q, k_new, v_new):
    b, ds, _ = q.shape
    n_pages = page_table.shape[1]
    u = min(SAMPLE_PAGES_PER_STEP, n_pages)
    _check_sample_shapes(n_pages, ds, u)
    nchunk = n_pages // u
    nrow = ds * H_A
    ck, cv = _page_view(cache_k), _page_view(cache_v)
    rbr, bk_last = _sample_tables(rel_bias, ds)

    def page_spec(j):
        return pl.BlockSpec((None, None, W_A, PAGE_SIZE),
                            lambda bi, c, pt: (pt[bi, c * u + j], layer, 0, 0))

    per_seq = lambda r, n: pl.BlockSpec((None, r, n), lambda bi, c, pt: (bi, 0, 0))
    const = lambda r, n: pl.BlockSpec((r, n), lambda bi, c, pt: (0, 0))
    probs, stats = pl.pallas_call(
        functools.partial(_moba_sample_k_kernel, u=u, nchunk=nchunk, ds=ds),
        grid_spec=pltpu.PrefetchScalarGridSpec(
            num_scalar_prefetch=1, grid=(b, nchunk),
            in_specs=[per_seq(ds, W_A), per_seq(ds, W_A), const(nrow, REL_BUCKETS), const(nrow, MOBA_BLOCK)]
                     + [page_spec(j) for j in range(u)],
            out_specs=[pl.BlockSpec((None, nchunk, nrow, u * PAGE_SIZE), lambda bi, c, pt: (bi, 0, 0, 0)),
                       per_seq(nrow, LANES)],
            scratch_shapes=[pltpu.VMEM((nrow, W_A), BF16), pltpu.VMEM((nchunk, nrow, u * PAGE_SIZE), F32),
                            pltpu.VMEM((W_A, LANES), F32)]),
        out_shape=[jax.ShapeDtypeStruct((b, nchunk, nrow, u * PAGE_SIZE), BF16),
                   jax.ShapeDtypeStruct((b, nrow, LANES), F32)],
        compiler_params=_cparams(("arbitrary", "arbitrary")),
        name="moba_sample_keys",
    )(page_table, q, k_new, rbr, bk_last, *([ck] * u))
    return _moba_sample_values(layer, page_table, cv, probs, stats, v_new, u)


def _moba_fused_kernel(pt_ref, rb_ref, bko_ref, bkp_ref, qt_ref, k_ref, vt_ref, km_ref,
                       q_ref, kn_ref, vn_ref, rbr_ref, bkl_ref, ck_ref, cv_ref, o_ref, p_out, st_out, y_ref,
                       bown, bprev, sel_ref, m_ref, acc_ref, sping, spong,
                       qbd_ref, s_ref, ks_ref, p_scr, st_scr, accv_ref, kbuf, vbuf, sems,
                       *, nb, u, nchunk, ds, layer, n_steps):
    p = pl.program_id(0)
    i = pl.program_id(1)
    step = p * nb + i
    b = step // nchunk
    c = step % nchunk
    par = b % 2
    slot = step % 2

    def page_copies(s, into):
        bs_, cs_ = s // nchunk, s % nchunk
        copies = []
        for j in range(u):
            kpg = pt_ref[bs_, cs_ * u + j]
            vpg = pt_ref[jnp.maximum(bs_ - 1, 0), cs_ * u + j]
            copies.append(pltpu.make_async_copy(ck_ref.at[kpg, layer], kbuf.at[into, j], sems.at[0, into]))
            copies.append(pltpu.make_async_copy(cv_ref.at[vpg, layer], vbuf.at[into, j], sems.at[1, into]))
        return copies

    @pl.when(step == 0)
    def _():
        for cp in page_copies(step, slot):
            cp.start()

    @pl.when(step + 1 < n_steps)
    def _():
        for cp in page_copies(step + 1, 1 - slot):
            cp.start()

    for cp in page_copies(step, slot):
        cp.wait()
    kpages = [kbuf.at[slot, j] for j in range(u)]
    vpages = [vbuf.at[slot, j] for j in range(u)]
    _sample_keys_step(c, q_ref, kn_ref, rbr_ref, bkl_ref, kpages, [p_out, p_scr.at[par]],
                      [st_out, st_scr.at[par]], qbd_ref, s_ref, ks_ref, u=u, nchunk=nchunk, ds=ds)

    @pl.when(b >= 1)
    def _():
        _sample_values_step(c, p_scr.at[1 - par, c], st_scr.at[1 - par], vn_ref, vpages, y_ref, accv_ref,
                            u=u, nchunk=nchunk, ds=ds)

    _moba_prompt_step(p, i, rb_ref, bko_ref, bkp_ref, qt_ref, k_ref, vt_ref, km_ref, o_ref,
                      bown, bprev, sel_ref, m_ref, acc_ref, sping, spong, nb=nb)


def _fused_pages_per_step(n_steps, bs, n_pages):
    if bs < 2 or (bs * n_pages) % n_steps:
        return None
    u = bs * n_pages // n_steps
    if u < 2 or u % 2 or u > SAMPLE_PAGES_PER_STEP or n_pages % u:
        return None
    return u


def _moba_fused(layer, rel_bias, qt, kb, vt, kmean, page_table, cache_k, cache_v, q, k_new, v_new, u):
    t = qt.shape[1]
    nb = t // MOBA_BLOCK
    blk = MOBA_BLOCK
    bs, ds, _ = q.shape
    n_pages = page_table.shape[1]
    _check_sample_shapes(n_pages, ds, u)
    nchunk = n_pages // u
    assert bs * nchunk == (H_A // 2) * nb
    nrow = ds * H_A
    ck, cv = _page_view(cache_k), _page_view(cache_v)
    rbr, bk_last = _sample_tables(rel_bias, ds)
    c_ = np.arange(blk)[:, None]
    r_ = np.arange(blk)[None, :]
    bk_own = np.where(c_ <= r_, _rel_bucket_np(r_ - c_), -1).astype(np.int32)
    bk_prev = _rel_bucket_np(blk + r_ - c_)

    seq_k = lambda p, i: (p * nb + i) // nchunk
    seq_v = lambda p, i: jnp.maximum(seq_k(p, i) - 1, 0)
    chunk = lambda p, i: (p * nb + i) % nchunk
    const = lambda shape: pl.BlockSpec(shape, lambda p, i, pt: (0,) * len(shape))
    per_k = lambda r, n: pl.BlockSpec((None, r, n), lambda p, i, pt: (seq_k(p, i), 0, 0))
    per_v = lambda r, n: pl.BlockSpec((None, r, n), lambda p, i, pt: (seq_v(p, i), 0, 0))
    smem = pl.BlockSpec(memory_space=pltpu.SMEM)
    hbm = pl.BlockSpec(memory_space=pl.ANY)
    ya, probs, stats, ys = pl.pallas_call(
        functools.partial(_moba_fused_kernel, nb=nb, u=u, nchunk=nchunk, ds=ds, layer=layer,
                          n_steps=(H_A // 2) * nb),
        grid_spec=pltpu.PrefetchScalarGridSpec(
            num_scalar_prefetch=1, grid=(H_A // 2, nb),
            in_specs=[smem, const((blk, blk)), const((blk, blk)),
                      pl.BlockSpec((LANES, blk), lambda p, i, pt: (p, i)),
                      pl.BlockSpec((nb, blk, LANES), lambda p, i, pt: (0, 0, p)),
                      pl.BlockSpec((nb, 2 * VROWS, blk), lambda p, i, pt: (0, p, 0)),
                      pl.BlockSpec((nb, LANES), lambda p, i, pt: (0, p)),
                      per_k(ds, W_A), per_k(ds, W_A), per_v(ds, W_A), const((nrow, REL_BUCKETS)),
                      const((nrow, blk)), hbm, hbm],
            out_specs=[pl.BlockSpec((blk, LANES), lambda p, i, pt: (i, p)),
                       pl.BlockSpec((None, nchunk, nrow, u * PAGE_SIZE), lambda p, i, pt: (seq_k(p, i), 0, 0, 0)),
                       per_k(nrow, LANES), per_v(ds, W_A)],
            scratch_shapes=[pltpu.VMEM((2, blk, blk), F32), pltpu.VMEM((2, blk, blk), F32),
                            pltpu.VMEM((2, nb, blk), F32), pltpu.VMEM((2, 1, blk), F32),
                            pltpu.VMEM((2, VROWS, blk), F32),
                            pltpu.VMEM((2, STEP_BLOCKS, blk, blk), F32), pltpu.VMEM((2, STEP_BLOCKS, blk, blk), F32),
                            pltpu.VMEM((nrow, W_A), BF16), pltpu.VMEM((nchunk, nrow, u * PAGE_SIZE), F32),
                            pltpu.VMEM((W_A, LANES), F32),
                            pltpu.VMEM((2, nchunk, nrow, u * PAGE_SIZE), BF16), pltpu.VMEM((2, nrow, LANES), F32),
                            pltpu.VMEM((nrow, W_A), F32),
                            pltpu.VMEM((2, u, W_A, PAGE_SIZE), F32), pltpu.VMEM((2, u, W_A, PAGE_SIZE), F32),
                            pltpu.SemaphoreType.DMA((2, 2))]),
        out_shape=[jax.ShapeDtypeStruct((t, W_A), F32),
                   jax.ShapeDtypeStruct((bs, nchunk, nrow, u * PAGE_SIZE), BF16),
                   jax.ShapeDtypeStruct((bs, nrow, LANES), F32),
                   jax.ShapeDtypeStruct((bs - 1, ds, W_A), F32)],
        compiler_params=_cparams(("arbitrary", "arbitrary")),
        name="moba_fused",
    )(page_table, rel_bias, jnp.asarray(bk_own), jnp.asarray(bk_prev), qt, kb.reshape(nb, blk, W_A), vt,
      kmean.reshape(nb, W_A), q, k_new, v_new, rbr, bk_last, ck, cv)
    y_last = _moba_sample_values(layer, page_table[bs - 1:], cv, probs[bs - 1:], stats[bs - 1:], v_new[bs - 1:], u)
    return ya, jnp.concatenate([ys, y_last], axis=0)


def _mix_kernel(x_ref, ya_ref, glu_ref, uc_ref, go_ref, hb_ref, hc_ref, cbw_ref, cbb_ref, lbg_ref, lbb_ref,
                ccw_ref, wo_ref, g1_ref, b1_ref, o_ref, extb, extc, shb, *, tm, shift, hbb, hbc, alpha):
    @pl.when(pl.program_id(0) == 0)
    def _():
        extb[0:hbb, :] = hb_ref[...]
        extc[0:hbc, :] = hc_ref[...]

    mix_a = _dot(ya_ref[...].astype(BF16), wo_ref[0:W_A, :])
    extb[hbb:hbb + tm, :] = glu_ref[...]
    extc[hbc:hbc + tm, :] = uc_ref[...]
    offb = hbb - (CONV_B - 1) * shift
    if shift != 1:
        tap = lambda w: extb[offb + w * shift:offb + w * shift + tm, :]
    else:
        for r in range(SUBLANES):
            rows = tm + SUBLANES * ((CONV_B - 1 - r) // SUBLANES)
            shb[r, 0:rows, :] = extb[offb + r:offb + r + rows, :]
        tap = lambda w: shb[w % SUBLANES, w - w % SUBLANES:w - w % SUBLANES + tm, :]
    cb = jnp.broadcast_to(cbb_ref[...], (tm, W_B))
    for w in range(CONV_B):
        cb = cb + tap(w) * cbw_ref[w:w + 1, :]
    yb = _layer_norm(cb, lbg_ref[...], lbb_ref[...])
    yb = yb * _sigmoid(yb)
    offc = hbc - (CONV_C - 1) * shift
    cc = extc[offc:offc + tm, :] * ccw_ref[0:1, :]
    for w in range(1, CONV_C):
        cc = cc + extc[offc + w * shift:offc + w * shift + tm, :] * ccw_ref[w:w + 1, :]
    yc = go_ref[...] * cc
    mix = (mix_a
           + _dot(yb.astype(BF16), wo_ref[W_A:W_A + W_B, :])
           + _dot(yc.astype(BF16), wo_ref[W_A + W_B:W_A + W_B + W_C, :]))
    o_ref[...] = _layer_norm(alpha * x_ref[...] + mix, g1_ref[...], b1_ref[...])
    nb_rows = extb[tm:tm + hbb, :]
    nc_rows = extc[tm:tm + hbc, :]
    extb[0:hbb, :] = nb_rows
    extc[0:hbc, :] = nc_rows


def _mix(x, ya, glu, uc, go, hist_b, hist_c, lw, *, shift, alpha):
    t = x.shape[0]
    tm = _row_tile(t)
    hbb, hbc = hist_b.shape[0], hist_c.shape[0]
    row = lambda n: pl.BlockSpec((tm, n), lambda i: (i, 0))
    return pl.pallas_call(
        functools.partial(_mix_kernel, tm=tm, shift=shift, hbb=hbb, hbc=hbc, alpha=alpha),
        grid=(t // tm,),
        in_specs=[row(D_MODEL), row(W_A), row(W_B), row(W_C), row(W_C), _full((hbb, W_B)), _full((hbc, W_C)),
                  _full((CONV_B, W_B)), _full((1, W_B)), _full((1, W_B)), _full((1, W_B)), _full((CONV_C, W_C)),
                  _full((D_MODEL, D_MODEL)), _full((1, D_MODEL)), _full((1, D_MODEL))],
        out_specs=row(D_MODEL),
        out_shape=jax.ShapeDtypeStruct((t, D_MODEL), F32),
        scratch_shapes=[pltpu.VMEM((hbb + tm, W_B), F32), pltpu.VMEM((hbc + tm, W_C), F32),
                        pltpu.VMEM((SUBLANES, tm + SUBLANES * ((CONV_B - 1) // SUBLANES), W_B), F32)],
        compiler_params=_cparams(("arbitrary",)),
        name="mix_ln1",
    )(x, ya, glu, uc, go, hist_b, hist_c, lw['convb_w'], lw['convb_b'], lw['lnb_g'], lw['lnb_b'],
      lw['convc_w'], lw['w_out'], lw['ln1_g'], lw['ln1_b'])


def _cross_kernel(x_ref, mk_ref, mv_ref, wq_ref, wo_ref, g_ref, b_ref, o_ref, qx_ref, acc_ref,
                  *, tm, n_groups, n_mem, interleaved, alpha):
    g = pl.program_id(1)

    @pl.when(g == 0)
    def _():
        qx_ref[...] = (_dot(x_ref[...].astype(BF16), wq_ref[...]) * (HD_X ** -0.5)).astype(BF16)
        acc_ref[...] = jnp.zeros_like(acc_ref)

    def head_mem(ref, h):
        if not interleaved:
            return ref[:, h * HD_X:(h + 1) * HD_X].astype(BF16)
        tiles = [ref[pl.ds(dt * H_X + h, n_mem, stride=MEM_ROWS), :] for dt in range(HD_X // LANES)]
        return jnp.concatenate(tiles, axis=1).astype(BF16)

    heads = []
    head_cols = [slice(h * HD_X, (h + 1) * HD_X) for h in range(H_X)]
    scores = [_dot_nt(qx_ref[:, head_cols[h]], head_mem(mk_ref, h)) for h in range(H_X)]
    for h, s in enumerate(scores):
        s = s - jnp.max(s, axis=-1, keepdims=True)
        e = jnp.exp(s)
        pr = e / jnp.sum(e, axis=-1, keepdims=True)
        heads.append(_dot(pr.astype(BF16), head_mem(mv_ref, h)))
    o = jnp.concatenate(heads, axis=-1)
    if n_groups == 1:
        acc_ref[...] = o
    else:
        rgrp = (pl.program_id(0) * tm + lax.broadcasted_iota(jnp.int32, (tm, 1), 0)) % n_groups
        acc_ref[...] = jnp.where(rgrp == g, o, acc_ref[...])

    @pl.when(g == n_groups - 1)
    def _():
        ox = _dot(acc_ref[...].astype(BF16), wo_ref[...])
        o_ref[...] = _layer_norm(alpha * x_ref[...] + ox, g_ref[...], b_ref[...])


def _interleave_mem(cache_mem):
    depth, b, n_mem = cache_mem.shape[:3]
    v = cache_mem.reshape(depth, b, n_mem, H_X, HD_X // LANES, LANES).transpose(0, 1, 2, 4, 3, 5)
    return v.reshape(depth, b, n_mem * MEM_ROWS, LANES)


def _cross(x, mem_k, mem_v, lw, *, alpha, layer=None):
    t = x.shape[0]
    tm = _row_tile(t)
    row = pl.BlockSpec((tm, D_MODEL), lambda i, g: (i, 0))
    interleaved = layer is not None
    if interleaved:
        n_groups, n_mem = mem_k.shape[1], mem_k.shape[2] // MEM_ROWS
        mem = pl.BlockSpec((None, None, n_mem * MEM_ROWS, LANES), lambda i, g: (layer, g, 0, 0))
    else:
        n_groups, n_mem, _ = mem_k.shape
        mem = pl.BlockSpec((None, n_mem, D_MODEL), lambda i, g: (g, 0, 0))
    cst = lambda r, n: pl.BlockSpec((r, n), lambda i, g: (0, 0))
    return pl.pallas_call(
        functools.partial(_cross_kernel, tm=tm, n_groups=n_groups, n_mem=n_mem, interleaved=interleaved,
                          alpha=alpha),
        grid=(t // tm, n_groups),
        in_specs=[row, mem, mem, cst(D_MODEL, D_MODEL), cst(D_MODEL, D_MODEL), cst(1, D_MODEL), cst(1, D_MODEL)],
        out_specs=row,
        out_shape=jax.ShapeDtypeStruct((t, D_MODEL), F32),
        scratch_shapes=[pltpu.VMEM((tm, D_MODEL), BF16), pltpu.VMEM((tm, D_MODEL), F32)],
        compiler_params=_cparams(("arbitrary", "arbitrary")),
        name="cross_ln2",
    )(x, mem_k, mem_v, lw['w_qx'], lw['w_ox'], lw['ln2_g'], lw['ln2_b'])


def _ffn_kernel(x_ref, hf_ref, wg_ref, wu_ref, cfw_ref, cfb_ref, wd_ref, g_ref, b_ref, o_ref, st_ref,
                carry, ext, acc_ref, *, tm, shift, hbf, alpha):
    @pl.when(pl.program_id(0) == 0)
    def _():
        carry[...] = hf_ref[...]

    xb = x_ref[...].astype(BF16)
    off = hbf - (CONV_F - 1) * shift
    n_chunks = D_FF // FF_CHUNK

    def gate_up(c):
        cols = slice(c * FF_CHUNK, (c + 1) * FF_CHUNK)
        return _dot(xb, wg_ref[:, cols]), _dot(xb, wu_ref[:, cols])

    nxt = gate_up(0)
    for c in range(n_chunks):
        cols = slice(c * FF_CHUNK, (c + 1) * FF_CHUNK)
        gate, up = nxt
        if c + 1 < n_chunks:
            nxt = gate_up(c + 1)
        ext[0:hbf, :] = carry[:, cols]
        ext[hbf:hbf + tm, :] = gate
        gf = jnp.broadcast_to(cfb_ref[:, cols], (tm, FF_CHUNK))
        for w in range(CONV_F):
            gf = gf + ext[off + w * shift:off + w * shift + tm, :] * cfw_ref[w:w + 1, cols]
        carry[:, cols] = ext[tm:tm + hbf, :]
        hid = gf * _sigmoid(gf) * up
        part = _dot(hid.astype(BF16), wd_ref[cols, :])
        if c == 0:
            acc_ref[...] = part
        else:
            acc_ref[...] += part
    o_ref[...] = _layer_norm(alpha * x_ref[...] + acc_ref[...], g_ref[...], b_ref[...])
    st_ref[...] = carry[...]


def _ffn(x, hist_f, lw, *, shift, alpha):
    t = x.shape[0]
    tm = _row_tile(t)
    hbf = hist_f.shape[0]
    row = pl.BlockSpec((tm, D_MODEL), lambda i: (i, 0))
    return pl.pallas_call(
        functools.partial(_ffn_kernel, tm=tm, shift=shift, hbf=hbf, alpha=alpha),
        grid=(t // tm,),
        in_specs=[row, _full((hbf, D_FF)), _full((D_MODEL, D_FF)), _full((D_MODEL, D_FF)),
                  _full((CONV_F, D_FF)), _full((1, D_FF)), _full((D_FF, D_MODEL)), _full((1, D_MODEL)),
                  _full((1, D_MODEL))],
        out_specs=[row, _full((hbf, D_FF))],
        out_shape=[jax.ShapeDtypeStruct((t, D_MODEL), F32), jax.ShapeDtypeStruct((hbf, D_FF), F32)],
        scratch_shapes=[pltpu.VMEM((hbf, D_FF), F32), pltpu.VMEM((hbf + tm, FF_CHUNK), F32),
                        pltpu.VMEM((tm, D_MODEL), F32)],
        compiler_params=_cparams(("arbitrary",)),
        name="ffn_ln3",
    )(x, hist_f, lw['w_gate'], lw['w_up'], lw['convf_w'], lw['convf_b'], lw['w_down'], lw['ln3_g'], lw['ln3_b'])


def _mem_proj_kernel(m_ref, wk_ref, wv_ref, k_ref, v_ref):
    mb = m_ref[...].astype(BF16)
    k_ref[...] = _dot(mb, wk_ref[...])
    v_ref[...] = _dot(mb, wv_ref[...])


def _mem_proj(mem, wk_bf, wv_bf):
    n = mem.shape[0]
    sd = jax.ShapeDtypeStruct((n, D_MODEL), F32)
    return pl.pallas_call(
        _mem_proj_kernel, out_shape=[sd, sd],
        compiler_params=pltpu.CompilerParams(vmem_limit_bytes=V7X_VMEM_LIMIT),
        name="mem_proj",
    )(mem, wk_bf, wv_bf)


def kernel(x_prompt, x_sample, mem_prompt, cache_k, cache_v, page_table, cache_mem_k, cache_mem_v, state_conv_b, state_conv_c, state_conv_f, rel_bias, w_in, w_out, convb_w, convb_b, lnb_g, lnb_b, convc_w, ln1_g, ln1_b, w_qx, w_kx, w_vx, w_ox, ln2_g, ln2_b, w_gate, w_up, convf_w, convf_b, w_down, ln3_g, ln3_b):
    bp, seq, _ = x_prompt.shape
    bs, ds, _ = x_sample.shape
    depth = w_in.shape[0]
    n_mem = mem_prompt.shape[1]
    alpha = _alpha(depth)
    assert bp == 1 and seq % MOBA_BLOCK == 0

    xp = x_prompt.reshape(seq, D_MODEL)
    xs = x_sample.transpose(1, 0, 2).reshape(ds * bs, D_MODEL)
    mem_p = mem_prompt.reshape(n_mem, D_MODEL)
    zeros_b = jnp.zeros((32, W_B), F32)
    zeros_c = jnp.zeros((8, W_C), F32)
    zeros_f = jnp.zeros((8, D_FF), F32)
    mem_k_sample = _interleave_mem(cache_mem_k)
    mem_v_sample = _interleave_mem(cache_mem_v)

    outs = {n: [] for n in ('kp', 'vp', 'ks', 'vs', 'mkp', 'mvp', 'cbp', 'cbs', 'ccp', 'ccs', 'cfp', 'cfs')}
    for l in range(depth):
        vec = lambda a: a[l].reshape(1, -1)
        lw = {'w_out': w_out[l].astype(BF16), 'convb_w': convb_w[l], 'convb_b': vec(convb_b),
              'lnb_g': vec(lnb_g), 'lnb_b': vec(lnb_b), 'convc_w': convc_w[l], 'ln1_g': vec(ln1_g),
              'ln1_b': vec(ln1_b), 'w_qx': w_qx[l].astype(BF16), 'w_ox': w_ox[l].astype(BF16),
              'ln2_g': vec(ln2_g), 'ln2_b': vec(ln2_b), 'w_gate': w_gate[l].astype(BF16),
              'w_up': w_up[l].astype(BF16), 'convf_w': convf_w[l], 'convf_b': vec(convf_b),
              'w_down': w_down[l].astype(BF16), 'ln3_g': vec(ln3_g), 'ln3_b': vec(ln3_b)}
        w_in_bf = w_in[l].astype(BF16)

        qt, kb, vt, k, v, glu, uc, go, kmean = _in_proj(xp, w_in_bf, prompt=True)
        q_s, k_s, v_s, glu_s, uc_s, go_s = _in_proj(xs, w_in_bf, prompt=False)
        seq_major = lambda a: a.reshape(ds, bs, -1).transpose(1, 0, 2)
        q_sm, k_sm, v_sm = seq_major(q_s), seq_major(k_s), seq_major(v_s)
        pages_per_step = _fused_pages_per_step((H_A // 2) * (seq // MOBA_BLOCK), bs, page_table.shape[1])
        if pages_per_step is None:
            ya = _moba_prompt(rel_bias, qt, kb, vt, kmean)
            ya_s = _moba_sample(l, rel_bias, page_table, cache_k, cache_v, q_sm, k_sm, v_sm)
        else:
            ya, ya_s = _moba_fused(l, rel_bias, qt, kb, vt, kmean, page_table, cache_k, cache_v,
                                   q_sm, k_sm, v_sm, pages_per_step)

        mk, mv = _mem_proj(mem_p, w_kx[l].astype(BF16), w_vx[l].astype(BF16))
        x1 = _mix(xp, ya, glu, uc, go, zeros_b, zeros_c, lw, shift=1, alpha=alpha)
        x2 = _cross(x1, mk[None], mv[None], lw, alpha=alpha)
        xp, st_f = _ffn(x2, zeros_f, lw, shift=1, alpha=alpha)
        from_rows = lambda a: a.reshape(1, H_A, HD_A, seq).transpose(0, 3, 1, 2)
        outs['kp'].append(from_rows(k))
        outs['vp'].append(from_rows(v))
        outs['mkp'].append(mk.reshape(1, n_mem, H_X, HD_X))
        outs['mvp'].append(mv.reshape(1, n_mem, H_X, HD_X))
        outs['cbp'].append(glu[None, seq - (CONV_B - 1):])
        outs['ccp'].append(uc[None, seq - (CONV_C - 1):])
        outs['cfp'].append(st_f[None, st_f.shape[0] - (CONV_F - 1):])

        hb = state_conv_b[l].transpose(1, 0, 2)
        hc = state_conv_c[l].transpose(1, 0, 2)
        hf = state_conv_f[l].transpose(1, 0, 2)
        glu, uc, go = glu_s, uc_s, go_s
        ya = ya_s.transpose(1, 0, 2).reshape(ds * bs, W_A)
        x1 = _mix(xs, ya, glu, uc, go, hb.reshape(-1, W_B), hc.reshape(-1, W_C), lw, shift=bs, alpha=alpha)
        x2 = _cross(x1, mem_k_sample, mem_v_sample, lw, alpha=alpha, layer=l)
        xs, st_f = _ffn(x2, hf.reshape(-1, D_FF), lw, shift=bs, alpha=alpha)
        outs['ks'].append(k_sm.reshape(bs, ds, H_A, HD_A))
        outs['vs'].append(v_sm.reshape(bs, ds, H_A, HD_A))
        tail = lambda hist, new, n: jnp.concatenate([hist, new.reshape(ds, bs, -1)], axis=0)[-n:].transpose(1, 0, 2)
        outs['cbs'].append(tail(hb, glu, CONV_B - 1))
        outs['ccs'].append(tail(hc, uc, CONV_C - 1))
        outs['cfs'].append(st_f.reshape(CONV_F - 1, bs, D_FF).transpose(1, 0, 2))

    y_prompt = xp.reshape(1, seq, D_MODEL)
    y_sample = xs.reshape(ds, bs, D_MODEL).transpose(1, 0, 2)
    st = lambda n, axis: jnp.stack(outs[n], axis=axis)
    return (y_prompt, y_sample, st('kp', 1), st('vp', 1), st('ks', 1), st('vs', 1), st('mkp', 0), st('mvp', 0),
            st('cbp', 0), st('cbs', 0), st('ccp', 0), st('ccs', 0), st('cfp', 0), st('cfs', 0))
```
